```python
import jax, jax.numpy as jnp
from jax import lax
import numpy as np

D_MODEL = 1024
BATCH = 16
SEQ = 256
DEPTH = 2
DEC_BATCH = 4
DEC_SEQ = 2048
PAST_LEN = 512

GRID_W = 64
N_MOD = 6
LN_EPS = 1e-5
CONV_WIDTH_A = D_MODEL
CONV_K = 31
HG_HEADS = 8
HG_DK = 128
HG_DV = 128
HG_WIDTH = HG_HEADS * HG_DK
HG_CHUNK = 16
SGU_WIDTH = D_MODEL
SGU_GROUPS = 8
SGU_GROUP_DIM = SGU_WIDTH // SGU_GROUPS
SGU_CHUNK = 128
FFN_HIDDEN = -(-8 * D_MODEL // (3 * 256)) * 256
DEEPNORM_ALPHA = (2 * DEPTH) ** 0.25
DEEPNORM_BETA = (8 * DEPTH) ** -0.25
A_IN = 2 * CONV_WIDTH_A
B_IN = 5 * HG_WIDTH
C_IN = 2 * SGU_WIDTH
GATE_IN = 3 * D_MODEL
D_IN = A_IN + B_IN + C_IN + GATE_IN

kernel_name = "hybrid_conv_hgrn2_sgu_diffusion_step"


def _layernorm(x, g, b):
    xf = x.astype(jnp.float32)
    mu = jnp.mean(xf, axis=-1, keepdims=True)
    var = jnp.mean(jnp.square(xf - mu), axis=-1, keepdims=True)
    y = (xf - mu) * lax.rsqrt(var + LN_EPS)
    return (y * g.astype(jnp.float32) + b.astype(jnp.float32)).astype(x.dtype)


def _sincos_2d(n_tokens, dim, dtype):
    rows = n_tokens // GRID_W
    t = jnp.arange(rows * GRID_W)
    r = (t // GRID_W).astype(jnp.float32)
    col = (t % GRID_W).astype(jnp.float32)
    nf = dim // 4
    omega = 1.0 / (10000.0 ** (jnp.arange(nf, dtype=jnp.float32) / nf))
    ar = r[:, None] * omega
    ac = col[:, None] * omega
    return jnp.concatenate([jnp.sin(ar), jnp.cos(ar), jnp.sin(ac), jnp.cos(ac)], axis=-1).astype(dtype)


def _conformer_conv(a_in, conv_w, conv_b, ln_g, ln_b, w_out):
    val, gate = jnp.split(a_in, 2, axis=-1)
    h = val * jax.nn.sigmoid(gate)
    h = lax.conv_general_dilated(h, conv_w[:, None, :], window_strides=(1,),
                                 padding=[(CONV_K // 2, CONV_K // 2)],
                                 dimension_numbers=('NWC', 'WIO', 'NWC'),
                                 feature_group_count=CONV_WIDTH_A) + conv_b
    h = jax.nn.silu(_layernorm(h, ln_g, ln_b))
    return h @ w_out


def _gla_chunked(q, k, v, logf, s0):
    bsz, t_len, n_h, _ = q.shape
    dv = v.shape[-1]
    n = t_len // HG_CHUNK

    def chunks(a):
        return a.reshape(bsz, n, HG_CHUNK, n_h, a.shape[-1]).transpose(0, 1, 3, 2, 4)

    q, k, v, logf = chunks(q), chunks(k), chunks(v), chunks(logf)
    b = jnp.cumsum(logf, axis=3)
    causal = jnp.tril(jnp.ones((HG_CHUNK, HG_CHUNK), dtype=bool))[:, :, None]
    diff = b[:, :, :, :, None, :] - b[:, :, :, None, :, :]
    decay = jnp.where(causal, jnp.exp(jnp.where(causal, diff, 0.0)), 0.0)
    scores = jnp.einsum('bnhtd,bnhsd,bnhtsd->bnhts', q, k, decay)
    o_intra = jnp.einsum('bnhts,bnhsv->bnhtv', scores, v)
    b_end = b[:, :, :, -1, :]
    k_to_end = k * jnp.exp(b_end[:, :, :, None, :] - b)
    ds = jnp.einsum('bnhsd,bnhsv->bnhdv', k_to_end, v)

    def step(s, inp):
        be, dsn = inp
        return jnp.exp(be)[..., None] * s + dsn, s

    s_fin, s_prev = lax.scan(step, s0, (jnp.moveaxis(b_end, 1, 0), jnp.moveaxis(ds, 1, 0)))
    s_prev = jnp.moveaxis(s_prev, 0, 1)
    o_inter = jnp.einsum('bnhtd,bnhdv->bnhtv', q * jnp.exp(b), s_prev)
    o = (o_intra + o_inter).transpose(0, 1, 3, 2, 4).reshape(bsz, t_len, n_h, dv)
    return o, s_fin


def _hgrn2(b_in, lb, norm_g, w_out, s0):
    q, zf, zb, i, g = jnp.split(b_in, 5, axis=-1)
    bsz, t_len, _ = q.shape

    def heads(a):
        return a.reshape(bsz, t_len, HG_HEADS, -1)

    lbf = lb.astype(jnp.float32)

    def forget(z):
        zf32 = z.astype(jnp.float32)
        f = lbf + (1.0 - lbf) * jax.nn.sigmoid(zf32)
        k = (1.0 - lbf) * jax.nn.sigmoid(-zf32)
        return heads(jnp.log(f)), heads(k)

    qh = heads(q.astype(jnp.float32))
    vh = heads(i.astype(jnp.float32))
    s0f = s0.astype(jnp.float32)
    logf_f, k_f = forget(zf)
    logf_b, k_b = forget(zb)
    o_f, s_f = _gla_chunked(qh, k_f, vh, logf_f, s0f[:, 0])
    o_b, s_b = _gla_chunked(qh[:, ::-1], k_b[:, ::-1], vh[:, ::-1], logf_b[:, ::-1], s0f[:, 1])
    o = o_f + o_b[:, ::-1]
    o = o * lax.rsqrt(jnp.mean(jnp.square(o), axis=-1, keepdims=True) + LN_EPS) \
        * norm_g.astype(jnp.float32).reshape(HG_HEADS, HG_DV)
    o = o.reshape(bsz, t_len, HG_WIDTH).astype(b_in.dtype) * jax.nn.silu(g)
    return o @ w_out, jnp.stack([s_f, s_b], axis=1).astype(b_in.dtype)


def _chunk_sgu(c_in, ln_g, ln_b, w_s, b_s, w_out):
    u, v = jnp.split(c_in, 2, axis=-1)
    v = _layernorm(v, ln_g, ln_b)
    bsz, t_len, _ = v.shape
    n = t_len // SGU_CHUNK
    vc = v.reshape(bsz, n, SGU_CHUNK, SGU_GROUPS, SGU_GROUP_DIM)
    mixed = jnp.einsum('gts,bnsgc->bntgc', w_s, vc) + b_s.T[None, None, :, :, None]
    return (u * mixed.reshape(bsz, t_len, SGU_WIDTH)) @ w_out


def _token_mixer(m, p, lb, s0):
    a_in, b_in, c_in, gates = jnp.split(m @ p['w_in'], [A_IN, A_IN + B_IN, A_IN + B_IN + C_IN], axis=-1)
    ya = _conformer_conv(a_in, p['conv_w'], p['conv_b'], p['conv_ln_g'], p['conv_ln_b'], p['w_a_out'])
    yb, s_out = _hgrn2(b_in, lb, p['hgrn_norm_g'], p['w_b_out'], s0)
    yc = _chunk_sgu(c_in, p['sgu_ln_g'], p['sgu_ln_b'], p['sgu_w'], p['sgu_b'], p['w_c_out'])
    ga, gb, gc = jnp.split(jax.nn.sigmoid(gates), 3, axis=-1)
    return (ga * ya + gb * yb + gc * yc) @ p['w_o'], s_out


def _swiglu(h, w_in, w_out):
    gte, up = jnp.split(h @ w_in, 2, axis=-1)
    return (jax.nn.silu(gte) * up) @ w_out


def _layer(x, cond, p, lb, s0):
    mod = jax.nn.silu(cond) @ p['w_ada'] + p['b_ada']
    sh1, sc1, g1, sh2, sc2, g2 = jnp.split(mod[:, None, :], N_MOD, axis=-1)
    mix, s_out = _token_mixer(x * (1 + sc1) + sh1, p, lb, s0)
    x = _layernorm(DEEPNORM_ALPHA * x + g1 * mix, p['ln1_g'], p['ln1_b'])
    ffn = _swiglu(x * (1 + sc2) + sh2, p['w_ffn_in'], p['w_ffn_out'])
    x = _layernorm(DEEPNORM_ALPHA * x + g2 * ffn, p['ln2_g'], p['ln2_b'])
    return x, s_out


def setup_inputs(seed: int = 0) -> dict:
    key = jax.random.key(seed)
    ks = jax.random.split(key, 32)
    nrm = jax.random.normal
    f32 = jnp.float32
    d = D_MODEL
    beta = DEEPNORM_BETA
    return {
        'x_prompt': nrm(ks[0], (BATCH, SEQ, d), f32),
        'x_sample': nrm(ks[1], (DEC_BATCH, DEC_SEQ, d), f32),
        'state_hgrn': 0.5 * nrm(ks[2], (DEC_BATCH, DEPTH, 2, HG_HEADS, HG_DK, HG_DV), f32),
        'c': nrm(ks[3], (DEC_BATCH, d), f32),
        'c_ctx': nrm(ks[4], (d,), f32),
        'ln_in_g': 1.0 + 0.01 * nrm(ks[5], (d,), f32),
        'ln_in_b': 0.01 * nrm(ks[6], (d,), f32),
        'w_ada': 0.2 * d ** -0.5 * nrm(ks[7], (DEPTH, d, N_MOD * d), f32),
        'b_ada': 0.01 * nrm(ks[8], (DEPTH, N_MOD * d), f32),
        'w_in': d ** -0.5 * nrm(ks[9], (DEPTH, d, D_IN), f32),
        'conv_w': CONV_K ** -0.5 * nrm(ks[10], (DEPTH, CONV_K, CONV_WIDTH_A), f32),
        'conv_b': 0.01 * nrm(ks[11], (DEPTH, CONV_WIDTH_A), f32),
        'conv_ln_g': 1.0 + 0.01 * nrm(ks[12], (DEPTH, CONV_WIDTH_A), f32),
        'conv_ln_b': 0.01 * nrm(ks[13], (DEPTH, CONV_WIDTH_A), f32),
        'w_a_out': beta * CONV_WIDTH_A ** -0.5 * nrm(ks[14], (DEPTH, CONV_WIDTH_A, d), f32),
        'hgrn_lb': nrm(ks[15], (DEPTH, HG_WIDTH), f32),
        'hgrn_norm_g': 1.0 + 0.01 * nrm(ks[16], (DEPTH, HG_WIDTH), f32),
        'w_b_out': beta * HG_WIDTH ** -0.5 * nrm(ks[17], (DEPTH, HG_WIDTH, d), f32),
        'sgu_ln_g': 1.0 + 0.01 * nrm(ks[18], (DEPTH, SGU_WIDTH), f32),
        'sgu_ln_b': 0.01 * nrm(ks[19], (DEPTH, SGU_WIDTH), f32),
        'sgu_w': SGU_CHUNK ** -0.5 * nrm(ks[20], (DEPTH, SGU_GROUPS, SGU_CHUNK, SGU_CHUNK), f32),
        'sgu_b': 1.0 + 0.01 * nrm(ks[21], (DEPTH, SGU_GROUPS, SGU_CHUNK), f32),
        'w_c_out': beta * SGU_WIDTH ** -0.5 * nrm(ks[22], (DEPTH, SGU_WIDTH, d), f32),
        'w_o': beta * d ** -0.5 * nrm(ks[23], (DEPTH, d, d), f32),
        'ln1_g': 1.0 + 0.01 * nrm(ks[24], (DEPTH, d), f32),
        'ln1_b': 0.01 * nrm(ks[25], (DEPTH, d), f32),
        'w_ffn_in': d ** -0.5 * nrm(ks[26], (DEPTH, d, 2 * FFN_HIDDEN), f32),
        'w_ffn_out': beta * FFN_HIDDEN ** -0.5 * nrm(ks[27], (DEPTH, FFN_HIDDEN, d), f32),
        'ln2_g': 1.0 + 0.01 * nrm(ks[28], (DEPTH, d), f32),
        'ln2_b': 0.01 * nrm(ks[29], (DEPTH, d), f32),
    }


def reference(x_prompt, x_sample, state_hgrn, c, c_ctx, ln_in_g, ln_in_b, w_ada, b_ada, w_in,
              conv_w, conv_b, conv_ln_g, conv_ln_b, w_a_out, hgrn_lb, hgrn_norm_g, w_b_out,
              sgu_ln_g, sgu_ln_b, sgu_w, sgu_b, w_c_out, w_o, ln1_g, ln1_b, w_ffn_in, w_ffn_out,
              ln2_g, ln2_b):
    lb_soft = jax.nn.softmax(hgrn_lb.astype(jnp.float32), axis=0)
    lower_bounds = jnp.cumsum(lb_soft, axis=0) - lb_soft[0]

    ctx = _layernorm(x_prompt, ln_in_g, ln_in_b)
    lat = _layernorm(x_sample + _sincos_2d(x_sample.shape[1], D_MODEL, x_sample.dtype), ln_in_g, ln_in_b)
    s0_ctx = jnp.zeros((x_prompt.shape[0], 2, HG_HEADS, HG_DK, HG_DV), x_prompt.dtype)
    cond_ctx = c_ctx[None, :]

    ctx_states = []
    for l in range(DEPTH):
        p = dict(w_ada=w_ada[l], b_ada=b_ada[l], w_in=w_in[l], conv_w=conv_w[l], conv_b=conv_b[l],
                 conv_ln_g=conv_ln_g[l], conv_ln_b=conv_ln_b[l], w_a_out=w_a_out[l],
                 hgrn_norm_g=hgrn_norm_g[l], w_b_out=w_b_out[l], sgu_ln_g=sgu_ln_g[l],
                 sgu_ln_b=sgu_ln_b[l], sgu_w=sgu_w[l], sgu_b=sgu_b[l], w_c_out=w_c_out[l],
                 w_o=w_o[l], ln1_g=ln1_g[l], ln1_b=ln1_b[l], w_ffn_in=w_ffn_in[l],
                 w_ffn_out=w_ffn_out[l], ln2_g=ln2_g[l], ln2_b=ln2_b[l])
        ctx, s_ctx = _layer(ctx, cond_ctx, p, lower_bounds[l], s0_ctx)
        ctx_states.append(s_ctx)
        lat, _ = _layer(lat, c, p, lower_bounds[l], state_hgrn[:, l])

    new_state_hgrn = jnp.stack(ctx_states, axis=1)
    return (ctx, lat, new_state_hgrn)
```

```python
import functools

import jax
import jax.numpy as jnp
import numpy as np
from jax import lax
from jax.experimental import pallas as pl
from jax.experimental.pallas import tpu as pltpu

LN_EPS = 1e-5
GRID_W = 64
LANES = 128
SUBLANES = 8
CHUNK = 128
SLABS = CHUNK // SUBLANES
N_LEVELS = 7
COND_ROWS = 8
VMEM_LIMIT = 56 * 1024 * 1024

F32 = jnp.float32
BF16 = jnp.bfloat16


def _cparams(*sem):
    return pltpu.CompilerParams(dimension_semantics=sem, vmem_limit_bytes=VMEM_LIMIT)


def _sigmoid(x):
    return 1.0 / (1.0 + jnp.exp(-x))


def _silu(x):
    return x * _sigmoid(x)


def _layernorm(x, g, b):
    mu = jnp.mean(x, axis=-1, keepdims=True)
    xc = x - mu
    var = jnp.mean(xc * xc, axis=-1, keepdims=True)
    return xc * lax.rsqrt(var + LN_EPS) * g + b


def _const_spec(shape):
    nd = len(shape)
    return pl.BlockSpec(shape, lambda *_: (0,) * nd, pipeline_mode=pl.Buffered(1))


def _ada_kernel(cond_ref, w_ref, b_ref, o_ref):
    s = _silu(cond_ref[...]).astype(BF16)
    o_ref[0] = jnp.dot(s, w_ref[0].astype(BF16), preferred_element_type=F32) + b_ref[0]


def _ada(cond, w_ada, b_ada):
    n_layers, d, n6 = w_ada.shape
    tn = n6 // 4
    return pl.pallas_call(
        _ada_kernel,
        grid=(n_layers, n6 // tn),
        in_specs=[
            pl.BlockSpec((COND_ROWS, d), lambda l, j: (0, 0)),
            pl.BlockSpec((1, d, tn), lambda l, j: (l, 0, j)),
            pl.BlockSpec((1, 1, tn), lambda l, j: (l, 0, j)),
        ],
        out_specs=pl.BlockSpec((1, COND_ROWS, tn), lambda l, j: (l, 0, j)),
        out_shape=jax.ShapeDtypeStruct((n_layers, COND_ROWS, n6), F32),
        compiler_params=_cparams("arbitrary", "arbitrary"),
        name="ada",
    )(cond, w_ada, b_ada.reshape(n_layers, 1, n6))


class _Geom:
    def __init__(self, n_ctx_seq, ctx_len, n_lat_seq, lat_len):
        self.n_ctx_seq, self.ctx_len = n_ctx_seq, ctx_len
        self.n_lat_seq, self.lat_len = n_lat_seq, lat_len
        self.t_ctx = n_ctx_seq * ctx_len
        self.t_lat = n_lat_seq * lat_len
        self.t = self.t_ctx + self.t_lat
        self.n_seq = n_ctx_seq + n_lat_seq

    def cond_row(self, row0):
        return jnp.where(row0 < self.t_ctx, 0, 1 + (row0 - self.t_ctx) // self.lat_len)

    def seq_of(self, row0):
        return jnp.where(row0 < self.t_ctx, row0 // self.ctx_len,
                         self.n_ctx_seq + (row0 - self.t_ctx) // self.lat_len)

    def pos_in_seq(self, row0):
        return jnp.where(row0 < self.t_ctx, row0 % self.ctx_len, (row0 - self.t_ctx) % self.lat_len)

    def seq_len(self, row0):
        return jnp.where(row0 < self.t_ctx, self.ctx_len, self.lat_len)


def _mod_rows(mod_ref, cid, d, idx):
    return [mod_ref[pl.ds(cid, 1), pl.ds(i * d, d)] for i in idx]


def _pre_kernel(xc_ref, xl_ref, pos_ref, g_ref, b_ref, mod_ref, x_ref, m_ref, *, geom, tm, d):
    i = pl.program_id(0)
    row0 = i * tm
    cid = geom.cond_row(row0)
    sh1, sc1 = _mod_rows(mod_ref, cid, d, (0, 1))

    def emit(x):
        xn = _layernorm(x, g_ref[...], b_ref[...])
        x_ref[...] = xn
        m_ref[...] = (xn * (1.0 + sc1) + sh1).astype(BF16)

    @pl.when(row0 < geom.t_ctx)
    def _():
        emit(xc_ref[...])

    @pl.when(row0 >= geom.t_ctx)
    def _():
        emit(xl_ref[...] + pos_ref[...])


def _pre(geom, x_ctx, x_lat, pos, g, b, mod0, tm):
    d = x_ctx.shape[-1]
    n_ctx_tiles = geom.t_ctx // tm
    pos_tiles = geom.lat_len // tm
    return pl.pallas_call(
        functools.partial(_pre_kernel, geom=geom, tm=tm, d=d),
        grid=(geom.t // tm,),
        in_specs=[
            pl.BlockSpec((tm, d), lambda i: (jnp.minimum(i, n_ctx_tiles - 1), 0)),
            pl.BlockSpec((tm, d), lambda i: (jnp.maximum(i - n_ctx_tiles, 0), 0)),
            pl.BlockSpec((tm, d), lambda i: (jnp.maximum(i - n_ctx_tiles, 0) % pos_tiles, 0)),
            _const_spec((1, d)), _const_spec((1, d)), _const_spec(mod0.shape),
        ],
        out_specs=[pl.BlockSpec((tm, d), lambda i: (i, 0))] * 2,
        out_shape=[jax.ShapeDtypeStruct((geom.t, d), F32), jax.ShapeDtypeStruct((geom.t, d), BF16)],
        compiler_params=_cparams("arbitrary"),
        name="pre",
    )(x_ctx, x_lat, pos, g, b, mod0)


N_BLK = 12
_S_VAL, _S_GATE, _S_G, _S_U, _S_V, _S_GA, _S_GB, _S_GC = range(8)
_H_Q, _H_ZF, _H_ZB, _H_I = range(4)
_FIRST_HEAD_BLK = 2
N_SLAB, N_KIND = 8, 4


def _slab_to_blk(s):
    return jnp.where(s < _FIRST_HEAD_BLK, s, s + N_KIND)


def _proj_tok_kernel(m_ref, w_ref, g_ref, b_ref, o_ref):
    s = pl.program_id(0)
    acc = jnp.dot(m_ref[...], w_ref[...], preferred_element_type=F32)

    @pl.when(jnp.logical_or(s == _S_VAL, s == _S_U))
    def _():
        o_ref[0] = acc

    @pl.when(jnp.logical_or(s == _S_GATE, s >= _S_GA))
    def _():
        o_ref[0] = _sigmoid(acc)

    @pl.when(s == _S_G)
    def _():
        o_ref[0] = _silu(acc)

    @pl.when(s == _S_V)
    def _():
        o_ref[0] = _layernorm(acc, g_ref[...], b_ref[...])


def _proj_tok(geom, m, w_in_bf16, sgu_g, sgu_b, layer, tm):
    d = m.shape[-1]
    return pl.pallas_call(
        _proj_tok_kernel,
        grid=(N_SLAB, geom.t // tm),
        in_specs=[
            pl.BlockSpec((tm, d), lambda s, i: (i, 0)),
            pl.BlockSpec((d, d), lambda s, i: (0, _slab_to_blk(s))),
            _const_spec((1, d)), _const_spec((1, d)),
        ],
        out_specs=pl.BlockSpec((1, tm, d), lambda s, i: (s, i, 0)),
        out_shape=jax.ShapeDtypeStruct((N_SLAB, geom.t, d), F32),
        compiler_params=_cparams("arbitrary", "arbitrary"),
        name=f"proj_tok{layer}",
    )(m, w_in_bf16, sgu_g, sgu_b)


def _proj_head_kernel(m_ref, w_ref, lb_ref, o_ref, *, layer, n_heads):
    k = pl.program_id(0)
    acc = jnp.dot(m_ref[...], w_ref[...], preferred_element_type=F32)

    def store(y):
        for h in range(n_heads):
            o_ref[0, h] = y[:, h * LANES:(h + 1) * LANES]

    @pl.when(jnp.logical_or(k == _H_Q, k == _H_I))
    def _():
        store(acc)

    @pl.when(jnp.logical_or(k == _H_ZF, k == _H_ZB))
    def _():
        lb_all = lb_ref[...]
        e = jnp.exp(lb_all - jnp.max(lb_all, axis=0, keepdims=True))
        soft = e / jnp.sum(e, axis=0, keepdims=True)
        lb = jnp.sum(soft[:layer + 1], axis=0, keepdims=True) - soft[0:1]
        store(lb + (1.0 - lb) * _sigmoid(acc))


def _proj_head(geom, m, w_in_bf16, hgrn_lb, layer, tm):
    d = m.shape[-1]
    n_heads = d // LANES
    return pl.pallas_call(
        functools.partial(_proj_head_kernel, layer=layer, n_heads=n_heads),
        grid=(N_KIND, geom.t // tm),
        in_specs=[
            pl.BlockSpec((tm, d), lambda k, i: (i, 0)),
            pl.BlockSpec((d, d), lambda k, i: (0, k + _FIRST_HEAD_BLK)),
            _const_spec(hgrn_lb.shape),
        ],
        out_specs=pl.BlockSpec((1, n_heads, tm, LANES), lambda k, i: (k, 0, i, 0)),
        out_shape=jax.ShapeDtypeStruct((N_KIND, n_heads, geom.t, LANES), F32),
        compiler_params=_cparams("arbitrary", "arbitrary"),
        name=f"proj_head{layer}",
    )(m, w_in_bf16, hgrn_lb)


def _level_table():
    p = np.arange(CHUNK)
    tok = SLABS * (p % SUBLANES) + p // SUBLANES
    t, s = tok[:, None], tok[None, :]
    x = t ^ s
    lvl = np.zeros((CHUNK, CHUNK), np.int32)
    for k in range(N_LEVELS):
        lvl = np.where((x >> k) > 0, k + 1, lvl)
    lvl = np.where(s > t, -1, lvl)
    return lvl.astype(np.int32)


def _row_bcast(x, r):
    return jnp.broadcast_to(x[r:r + 1, :], x.shape)


def _sublane_gather(src, rows_from, riota):
    out = jnp.ones_like(src)
    for r, rf in enumerate(rows_from):
        if rf is not None:
            out = jnp.where(riota == r, _row_bcast(src, rf), out)
    return out


def _hgrn_head(q, f, v, st, lvl, *, reverse):
    kk_vec = [1.0 - fj for fj in f]
    riota = lax.broadcasted_iota(jnp.int32, (SUBLANES, LANES), 0)
    tgt = list(f)
    src = [jnp.ones_like(fj) for fj in f]

    def cat_bf16(slabs):
        return jnp.concatenate(slabs, axis=0).astype(BF16)

    def scores(qs, ks):
        return lax.dot_general(cat_bf16(qs), cat_bf16(ks), (((1,), (1,)), ((), ())),
                               preferred_element_type=F32)

    a = jnp.where(lvl == 0, scores(q, kk_vec), 0.0)
    for k in range(N_LEVELS):
        p = scores([q[j] * tgt[j] for j in range(SLABS)], [kk_vec[j] * src[j] for j in range(SLABS)])
        a = jnp.where(lvl == k + 1, p, a)
        if k < 4:
            bit, low = 1 << k, (1 << k) - 1
            new_tgt, new_src = list(tgt), list(src)
            for j in range(SLABS):
                in_far_half = bool(j & bit) != reverse
                if in_far_half:
                    sib_edge = ((j & ~bit) | low) if not reverse else ((j | bit) & ~low)
                    new_tgt[j] = tgt[j] * tgt[sib_edge]
                else:
                    sib_edge = ((j | bit) | low) if not reverse else ((j & ~bit) & ~low)
                    new_src[j] = src[j] * tgt[sib_edge]
            tgt, src = new_tgt, new_src
        else:
            bit, low = 1 << (k - 4), (1 << (k - 4)) - 1
            edge = tgt[SLABS - 1] if not reverse else tgt[0]
            rows_t, rows_s = [], []
            for r in range(SUBLANES):
                in_far_half = bool(r & bit) != reverse
                if in_far_half:
                    rows_t.append(((r & ~bit) | low) if not reverse else ((r | bit) & ~low))
                    rows_s.append(None)
                else:
                    rows_t.append(None)
                    rows_s.append(((r | bit) | low) if not reverse else ((r & ~bit) & ~low))
            fac_t = _sublane_gather(edge, rows_t, riota)
            fac_s = _sublane_gather(edge, rows_s, riota)
            tgt = [x * fac_t for x in tgt]
            src = [x * fac_s for x in src]

    v_cat = cat_bf16(v)
    o = jnp.dot(a.astype(BF16), v_cat, preferred_element_type=F32)
    o += lax.dot_general(cat_bf16([q[j] * tgt[j] for j in range(SLABS)]), st.astype(BF16),
                         (((1,), (1,)), ((), ())), preferred_element_type=F32)
    total = tgt[SLABS - 1][SUBLANES - 1:SUBLANES, :] if not reverse else tgt[0][0:1, :]
    upd = lax.dot_general(v_cat, cat_bf16([kk_vec[j] * src[j] for j in range(SLABS)]),
                          (((0,), (0,)), ((), ())), preferred_element_type=F32)
    return o, st * total + upd


def _hgrn_kernel(q_ref, f_ref, v_ref, lvl_ref, s0_ref, o_ref, sf_ref, st_ref, *, geom, reverse, n_heads):
    c = pl.program_id(0)
    n_chunks = geom.t // CHUNK
    ce = (n_chunks - 1 - c) if reverse else c
    row0 = ce * CHUNK
    pos = geom.pos_in_seq(row0)
    is_first = (pos == geom.seq_len(row0) - CHUNK) if reverse else (pos == 0)
    is_last = (pos == 0) if reverse else (pos == geom.seq_len(row0) - CHUNK)

    @pl.when(is_first)
    def _():
        for h in range(n_heads):
            st_ref[h] = s0_ref[0, h].T

    lvl = lvl_ref[...]
    for h in range(n_heads):

        def slabs(ref):
            return [ref.at[0, h][pl.ds(j, SUBLANES, stride=SLABS), :] for j in range(SLABS)]

        o, st_new = _hgrn_head(slabs(q_ref), slabs(f_ref), slabs(v_ref), st_ref[h], lvl, reverse=reverse)
        st_ref[h] = st_new
        for j in range(SLABS):
            o_ref.at[h][pl.ds(j, SUBLANES, stride=SLABS), :] = o[j * SUBLANES:(j + 1) * SUBLANES, :]

    @pl.when(is_last)
    def _():
        for h in range(n_heads):
            sf_ref[0, h] = st_ref[h].T


def _hgrn(geom, proj_h, s0, lvl, *, reverse, layer):
    _, n_heads, t, _ = proj_h.shape
    n_chunks = t // CHUNK

    def chunk(c):
        return (n_chunks - 1 - c) if reverse else c

    def blk(kind):
        return pl.BlockSpec((1, n_heads, CHUNK, LANES), lambda c: (kind, 0, chunk(c), 0))

    state_spec = pl.BlockSpec((1, n_heads, LANES, LANES), lambda c: (geom.seq_of(chunk(c) * CHUNK), 0, 0, 0))
    return pl.pallas_call(
        functools.partial(_hgrn_kernel, geom=geom, reverse=reverse, n_heads=n_heads),
        grid=(n_chunks,),
        in_specs=[blk(_H_Q), blk(_H_ZB if reverse else _H_ZF), blk(_H_I),
                  _const_spec((CHUNK, CHUNK)), state_spec],
        out_specs=[pl.BlockSpec((n_heads, CHUNK, LANES), lambda c: (0, chunk(c), 0)), state_spec],
        out_shape=[jax.ShapeDtypeStruct((n_heads, t, LANES), F32),
                   jax.ShapeDtypeStruct((geom.n_seq, n_heads, LANES, LANES), F32)],
        scratch_shapes=[pltpu.VMEM((n_heads, LANES, LANES), F32)],
        compiler_params=_cparams("arbitrary"),
        name=f"hgrn{layer}{'b' if reverse else 'f'}",
    )(proj_h, proj_h, proj_h, lvl, s0)


def _mix_kernel(val_ref, gate_ref, valp_ref, gatep_ref, valn_ref, gaten_ref,
                sg_ref, u_ref, vn_ref, ga_ref, gb_ref, gc_ref, of_ref, ob_ref, x_ref, mod_ref,
                cw_ref, cb_ref, clg_ref, clb_ref, wa_ref, ng_ref, wb_ref, sw_ref, sb_ref, wc_ref,
                wo_ref, l1g_ref, l1b_ref,
                x1_ref, m2_ref, hpad_ref, *, geom, tm, d, halo, conv_k, alpha):
    i = pl.program_id(0)
    row0 = i * tm
    cid = geom.cond_row(row0)
    pos = geom.pos_in_seq(row0)
    has_prev = pos > 0
    has_next = pos + tm < geom.seq_len(row0)

    hpad_ref[pl.ds(halo, tm), :] = val_ref[0] * gate_ref[0]
    hpad_ref[pl.ds(0, halo), :] = jnp.where(has_prev, valp_ref[0] * gatep_ref[0], 0.0)
    hpad_ref[pl.ds(halo + tm, halo), :] = jnp.where(has_next, valn_ref[0] * gaten_ref[0], 0.0)
    half = conv_k // 2
    acc = jnp.zeros((tm, d), F32) + cb_ref[...]
    for k in range(conv_k):
        acc += hpad_ref[pl.ds(halo - half + k, tm), :] * cw_ref[k:k + 1, :]
    ha = _silu(_layernorm(acc, clg_ref[...], clb_ref[...]))
    mix = ga_ref[0] * jnp.dot(ha.astype(BF16), wa_ref[...], preferred_element_type=F32)

    parts = []
    for h in range(d // LANES):
        oh = of_ref[h] + ob_ref[h]
        parts.append(oh * lax.rsqrt(jnp.mean(oh * oh, axis=-1, keepdims=True) + LN_EPS))
    ob = jnp.concatenate(parts, axis=-1) * ng_ref[...] * sg_ref[0]
    mix += gb_ref[0] * jnp.dot(ob.astype(BF16), wb_ref[...], preferred_element_type=F32)

    vn = vn_ref[0].astype(BF16)
    rows = []
    for ch in range(tm // LANES):
        cols = []
        for g in range(d // LANES):
            vg = vn[ch * LANES:(ch + 1) * LANES, g * LANES:(g + 1) * LANES]
            cols.append(jnp.dot(sw_ref[g], vg, preferred_element_type=F32) + sb_ref[g])
        rows.append(jnp.concatenate(cols, axis=-1))
    mixed = jnp.concatenate(rows, axis=0)
    yc = jnp.dot((u_ref[0] * mixed).astype(BF16), wc_ref[...], preferred_element_type=F32)
    mix += gc_ref[0] * yc

    y = jnp.dot(mix.astype(BF16), wo_ref[...], preferred_element_type=F32)
    g1, sh2, sc2 = _mod_rows(mod_ref, cid, d, (2, 3, 4))
    x1 = _layernorm(alpha * x_ref[...] + g1 * y, l1g_ref[...], l1b_ref[...])
    x1_ref[...] = x1
    m2_ref[...] = (x1 * (1.0 + sc2) + sh2).astype(BF16)


def _mix(geom, proj, o_f, o_b, x, mod_l, p, layer, tm, alpha):
    _, t, d = proj.shape
    n_heads = d // LANES
    halo = 16
    conv_k = p["conv_w"].shape[0]
    n_g = p["sgu_w"].shape[0]
    assert conv_k // 2 <= halo and tm % halo == 0 and tm % LANES == 0 and n_g * LANES == d
    r = tm // halo
    n_halo = t // halo

    def blk(b):
        return pl.BlockSpec((1, tm, d), lambda i: (b, i, 0))

    def prev(b):
        return pl.BlockSpec((1, halo, d), lambda i: (b, jnp.maximum(i * r - 1, 0), 0))

    def nxt(b):
        return pl.BlockSpec((1, halo, d), lambda i: (b, jnp.minimum((i + 1) * r, n_halo - 1), 0))

    tok = pl.BlockSpec((tm, d), lambda i: (i, 0))
    head_tok = pl.BlockSpec((n_heads, tm, LANES), lambda i: (0, i, 0))
    vec = _const_spec((1, d))
    mat = _const_spec((d, d))
    in_specs = [blk(_S_VAL), blk(_S_GATE), prev(_S_VAL), prev(_S_GATE), nxt(_S_VAL), nxt(_S_GATE),
                blk(_S_G), blk(_S_U), blk(_S_V), blk(_S_GA), blk(_S_GB), blk(_S_GC),
                head_tok, head_tok, tok, _const_spec(mod_l.shape),
                _const_spec((conv_k, d)), vec, vec, vec, mat, vec, mat,
                _const_spec((n_g, LANES, LANES)), _const_spec((n_g, LANES, 1)), mat, mat, vec, vec]
    return pl.pallas_call(
        functools.partial(_mix_kernel, geom=geom, tm=tm, d=d, halo=halo, conv_k=conv_k, alpha=alpha),
        grid=(t // tm,),
        in_specs=in_specs,
        out_specs=[tok, tok],
        out_shape=[jax.ShapeDtypeStruct((t, d), F32), jax.ShapeDtypeStruct((t, d), BF16)],
        scratch_shapes=[pltpu.VMEM((tm + 2 * halo, d), F32)],
        compiler_params=_cparams("arbitrary"),
        name=f"mix{layer}",
    )(*([proj] * 12), o_f, o_b, x, mod_l,
      p["conv_w"], p["conv_b"], p["conv_ln_g"], p["conv_ln_b"], p["w_a_out"], p["hgrn_norm_g"], p["w_b_out"],
      p["sgu_w"], p["sgu_b"], p["w_c_out"], p["w_o"], p["ln1_g"], p["ln1_b"])


def _ffn_kernel(m2_ref, x1_ref, mod_ref, modn_ref, wi_ref, wo_ref, g_ref, b_ref, x2_ref, mn_ref,
                *, geom, tm, d, f_hidden, f_chunk, alpha, emit_next):
    i = pl.program_id(0)
    cid = geom.cond_row(i * tm)
    m2 = m2_ref[...]
    acc = jnp.zeros((tm, d), F32)
    for c0 in range(0, f_hidden, f_chunk):
        gte = jnp.dot(m2, wi_ref[:, pl.ds(c0, f_chunk)], preferred_element_type=F32)
        up = jnp.dot(m2, wi_ref[:, pl.ds(f_hidden + c0, f_chunk)], preferred_element_type=F32)
        acc += jnp.dot((_silu(gte) * up).astype(BF16), wo_ref[pl.ds(c0, f_chunk), :],
                       preferred_element_type=F32)
    (g2,) = _mod_rows(mod_ref, cid, d, (5,))
    x2 = _layernorm(alpha * x1_ref[...] + g2 * acc, g_ref[...], b_ref[...])
    x2_ref[...] = x2
    if emit_next:
        sh1, sc1 = _mod_rows(modn_ref, cid, d, (0, 1))
        mn_ref[...] = (x2 * (1.0 + sc1) + sh1).astype(BF16)
    else:
        mn_ref[...] = jnp.zeros_like(mn_ref)


def _ffn(geom, m2, x1, mod_l, mod_next, p, layer, tm, alpha, emit_next):
    t, d = x1.shape
    f_hidden = p["w_ffn_out"].shape[0]
    f_chunk = f_hidden // 2
    assert f_chunk % LANES == 0
    tok = pl.BlockSpec((tm, d), lambda i: (i, 0))
    return pl.pallas_call(
        functools.partial(_ffn_kernel, geom=geom, tm=tm, d=d, f_hidden=f_hidden, f_chunk=f_chunk,
                          alpha=alpha, emit_next=emit_next),
        grid=(t // tm,),
        in_specs=[tok, tok, _const_spec(mod_l.shape), _const_spec(mod_next.shape),
                  _const_spec((d, 2 * f_hidden)), _const_spec((f_hidden, d)),
                  _const_spec((1, d)), _const_spec((1, d))],
        out_specs=[tok, tok],
        out_shape=[jax.ShapeDtypeStruct((t, d), F32), jax.ShapeDtypeStruct((t, d), BF16)],
        compiler_params=_cparams("arbitrary"),
        name=f"ffn{layer}",
    )(m2, x1, mod_l, mod_next, p["w_ffn_in"], p["w_ffn_out"], p["ln2_g"], p["ln2_b"])


def _sincos_2d(n_tokens, dim):
    rows = n_tokens // GRID_W
    t = jnp.arange(rows * GRID_W)
    r = (t // GRID_W).astype(F32)
    col = (t % GRID_W).astype(F32)
    nf = dim // 4
    omega = 1.0 / (10000.0 ** (jnp.arange(nf, dtype=F32) / nf))
    ar = r[:, None] * omega
    ac = col[:, None] * omega
    return jnp.concatenate([jnp.sin(ar), jnp.cos(ar), jnp.sin(ac), jnp.cos(ac)], axis=-1)


def kernel(x_prompt, x_sample, state_hgrn, c, c_ctx, ln_in_g, ln_in_b, w_ada, b_ada, w_in, conv_w, conv_b, conv_ln_g, conv_ln_b, w_a_out, hgrn_lb, hgrn_norm_g, w_b_out, sgu_ln_g, sgu_ln_b, sgu_w, sgu_b, w_c_out, w_o, ln1_g, ln1_b, w_ffn_in, w_ffn_out, ln2_g, ln2_b):
    n_ctx_seq, ctx_len, d = x_prompt.shape
    n_lat_seq, lat_len, _ = x_sample.shape
    n_layers = w_in.shape[0]
    n_heads = state_hgrn.shape[3]
    geom = _Geom(n_ctx_seq, ctx_len, n_lat_seq, lat_len)
    alpha = float((2 * n_layers) ** 0.25)
    assert d == n_heads * LANES and w_in.shape[2] == N_BLK * d
    assert 1 + n_lat_seq <= COND_ROWS and ctx_len % CHUNK == 0 and lat_len % CHUNK == 0

    tm_rows = min(ctx_len, 256)
    tm_big = min(2 * ctx_len, 512)
    assert ctx_len % tm_rows == 0 and lat_len % tm_rows == 0
    assert geom.t_ctx % tm_big == 0 and lat_len % tm_big == 0 and (tm_big <= ctx_len or tm_big % ctx_len == 0)

    cond = jnp.zeros((COND_ROWS, d), F32).at[0].set(c_ctx).at[1:1 + n_lat_seq].set(c)
    mod = _ada(cond, w_ada, b_ada)

    pos = _sincos_2d(lat_len, d)
    x, m = _pre(geom, x_prompt.reshape(geom.t_ctx, d), x_sample.reshape(geom.t_lat, d), pos,
                ln_in_g.reshape(1, d), ln_in_b.reshape(1, d), mod[0], tm_big)

    lvl_f = jnp.asarray(_level_table())
    lvl_b = jnp.asarray(_level_table().T.copy())
    zeros_ctx = jnp.zeros((n_ctx_seq, n_heads, LANES, LANES), F32)
    states = []
    for l in range(n_layers):
        p = dict(conv_w=conv_w[l], conv_b=conv_b[l].reshape(1, d), conv_ln_g=conv_ln_g[l].reshape(1, d),
                 conv_ln_b=conv_ln_b[l].reshape(1, d), w_a_out=w_a_out[l].astype(BF16),
                 hgrn_norm_g=hgrn_norm_g[l].reshape(1, d), w_b_out=w_b_out[l].astype(BF16),
                 sgu_w=sgu_w[l].astype(BF16), sgu_b=sgu_b[l][:, :, None], w_c_out=w_c_out[l].astype(BF16),
                 w_o=w_o[l].astype(BF16), ln1_g=ln1_g[l].reshape(1, d), ln1_b=ln1_b[l].reshape(1, d),
                 w_ffn_in=w_ffn_in[l].astype(BF16), w_ffn_out=w_ffn_out[l].astype(BF16),
                 ln2_g=ln2_g[l].reshape(1, d), ln2_b=ln2_b[l].reshape(1, d))
        w_in_l = w_in[l].astype(BF16)
        proj = _proj_tok(geom, m, w_in_l, sgu_ln_g[l].reshape(1, d), sgu_ln_b[l].reshape(1, d), l, tm_big)
        proj_h = _proj_head(geom, m, w_in_l, hgrn_lb, l, tm_big)
        s0_f = jnp.concatenate([zeros_ctx, state_hgrn[:, l, 0]], axis=0)
        s0_b = jnp.concatenate([zeros_ctx, state_hgrn[:, l, 1]], axis=0)
        o_f, sf_f = _hgrn(geom, proj_h, s0_f, lvl_f, reverse=False, layer=l)
        o_b, sf_b = _hgrn(geom, proj_h, s0_b, lvl_b, reverse=True, layer=l)
        states.append(jnp.stack([sf_f[:n_ctx_seq], sf_b[:n_ctx_seq]], axis=1))
        x1, m2 = _mix(geom, proj, o_f, o_b, x, mod[l], p, l, tm_rows, alpha)
        last = l == n_layers - 1
        x, m = _ffn(geom, m2, x1, mod[l], mod[l if last else l + 1], p, l, tm_big, alpha, not last)

    y_prompt = x[:geom.t_ctx].reshape(x_prompt.shape)
    y_sample = x[geom.t_ctx:].reshape(x_sample.shape)
    return y_prompt, y_sample, jnp.stack(states, axis=1)
```

```python
import functools

import jax
import jax.numpy as jnp
import numpy as np
from jax import lax
from jax.experimental import pallas as pl
from jax.experimental.pallas import tpu as pltpu

LN_EPS = 1e-5
GRID_W = 64
LANES = 128
SUBLANES = 8
CHUNK = 128
SLABS = CHUNK // SUBLANES
N_LEVELS = 7
COND_ROWS = 8
VMEM_LIMIT = 56 * 1024 * 1024

F32 = jnp.float32
BF16 = jnp.bfloat16


def _cparams(*sem):
    return pltpu.CompilerParams(dimension_semantics=sem, vmem_limit_bytes=VMEM_LIMIT)


def _sigmoid(x):
    return 1.0 / (1.0 + jnp.exp(-x))


def _silu(x):
    return x * _sigmoid(x)


def _layernorm(x, g, b):
    mu = jnp.mean(x, axis=-1, keepdims=True)
    xc = x - mu
    var = jnp.mean(xc * xc, axis=-1, keepdims=True)
    return xc * lax.rsqrt(var + LN_EPS) * g + b


def _const_spec(shape):
    nd = len(shape)
    return pl.BlockSpec(shape, lambda *_: (0,) * nd, pipeline_mode=pl.Buffered(1))


def _layer_spec(shape, layer):
    nd = len(shape)
    return pl.BlockSpec((None,) + tuple(shape), lambda *_: (layer,) + (0,) * nd, pipeline_mode=pl.Buffered(1))


def _slab_tokens():
    p = np.arange(CHUNK)
    return SLABS * (p % SUBLANES) + p // SUBLANES


def _to_slab_order(x, n_tokens_axis=0):
    t = x.shape[0]
    rest = x.shape[1:]
    return x.reshape((t // CHUNK, SUBLANES, SLABS) + rest).swapaxes(1, 2).reshape((t,) + rest)


def _from_slab_order(x):
    t = x.shape[0]
    rest = x.shape[1:]
    return x.reshape((t // CHUNK, SLABS, SUBLANES) + rest).swapaxes(1, 2).reshape((t,) + rest)


def _ada_kernel(cond_ref, w_ref, b_ref, o_ref):
    s = _silu(cond_ref[...]).astype(BF16)
    o_ref[0] = jnp.dot(s, w_ref[0].astype(BF16), preferred_element_type=F32) + b_ref[0]


def _ada(cond, w_ada, b_ada):
    n_layers, d, n6 = w_ada.shape
    tn = n6 // 4
    return pl.pallas_call(
        _ada_kernel,
        grid=(n_layers, n6 // tn),
        in_specs=[
            pl.BlockSpec((COND_ROWS, d), lambda l, j: (0, 0)),
            pl.BlockSpec((1, d, tn), lambda l, j: (l, 0, j)),
            pl.BlockSpec((1, 1, tn), lambda l, j: (l, 0, j)),
        ],
        out_specs=pl.BlockSpec((1, COND_ROWS, tn), lambda l, j: (l, 0, j)),
        out_shape=jax.ShapeDtypeStruct((n_layers, COND_ROWS, n6), F32),
        compiler_params=_cparams("arbitrary", "arbitrary"),
        name="ada",
    )(cond, w_ada, b_ada.reshape(n_layers, 1, n6))


class _Geom:
    def __init__(self, n_ctx_seq, ctx_len, n_lat_seq, lat_len):
        self.n_ctx_seq, self.ctx_len = n_ctx_seq, ctx_len
        self.n_lat_seq, self.lat_len = n_lat_seq, lat_len
        self.t_ctx = n_ctx_seq * ctx_len
        self.t_lat = n_lat_seq * lat_len
        self.t = self.t_ctx + self.t_lat
        self.n_seq = n_ctx_seq + n_lat_seq

    def cond_row(self, row0):
        return jnp.where(row0 < self.t_ctx, 0, 1 + (row0 - self.t_ctx) // self.lat_len)

    def seq_of(self, row0):
        return jnp.where(row0 < self.t_ctx, row0 // self.ctx_len,
                         self.n_ctx_seq + (row0 - self.t_ctx) // self.lat_len)

    def pos_in_seq(self, row0):
        return jnp.where(row0 < self.t_ctx, row0 % self.ctx_len, (row0 - self.t_ctx) % self.lat_len)

    def seq_len(self, row0):
        return jnp.where(row0 < self.t_ctx, self.ctx_len, self.lat_len)


def _mod_rows(mod_ref, cid, d, idx):
    return [mod_ref[pl.ds(cid, 1), pl.ds(i * d, d)] for i in idx]


def _pre_kernel(xc_ref, xl_ref, pos_ref, g_ref, b_ref, mod_ref, x_ref, m_ref, *, geom, tm, d):
    i = pl.program_id(0)
    row0 = i * tm
    cid = geom.cond_row(row0)
    sh1, sc1 = _mod_rows(mod_ref, cid, d, (0, 1))

    def emit(x):
        xn = _layernorm(x, g_ref[...], b_ref[...])
        x_ref[...] = xn
        m_ref[...] = (xn * (1.0 + sc1) + sh1).astype(BF16)

    @pl.when(row0 < geom.t_ctx)
    def _():
        emit(xc_ref[...])

    @pl.when(row0 >= geom.t_ctx)
    def _():
        emit(xl_ref[...] + pos_ref[...])


def _pre(geom, x_ctx, x_lat, pos, g, b, mod, tm):
    d = x_ctx.shape[-1]
    n_ctx_tiles = geom.t_ctx // tm
    pos_tiles = geom.lat_len // tm
    return pl.pallas_call(
        functools.partial(_pre_kernel, geom=geom, tm=tm, d=d),
        grid=(geom.t // tm,),
        in_specs=[
            pl.BlockSpec((tm, d), lambda i: (jnp.minimum(i, n_ctx_tiles - 1), 0)),
            pl.BlockSpec((tm, d), lambda i: (jnp.maximum(i - n_ctx_tiles, 0), 0)),
            pl.BlockSpec((tm, d), lambda i: (jnp.maximum(i - n_ctx_tiles, 0) % pos_tiles, 0)),
            _const_spec((1, d)), _const_spec((1, d)), _layer_spec(mod.shape[1:], 0),
        ],
        out_specs=[pl.BlockSpec((tm, d), lambda i: (i, 0))] * 2,
        out_shape=[jax.ShapeDtypeStruct((geom.t, d), F32), jax.ShapeDtypeStruct((geom.t, d), BF16)],
        compiler_params=_cparams("arbitrary"),
        name="pre",
    )(x_ctx, x_lat, pos, g, b, mod)


_W_VAL, _W_GATE, _W_Q, _W_ZF, _W_ZB, _W_I, _W_G, _W_U, _W_V, _W_GA, _W_GB, _W_GC = range(12)
N_BLK = 12
_S_H, _S_U, _S_V, _S_GA, _S_GB, _S_GC = range(6)
_TOK_BLKS = (_W_VAL, _W_U, _W_V, _W_GA, _W_GB, _W_GC)
_K_Q, _K_I, _K_SG = range(3)
_HB_BLKS = (_W_Q, _W_I, _W_G)
_HF_BLKS = (_W_ZF, _W_ZB)


def _blk_lookup(blks, s):
    out = blks[-1]
    for k in range(len(blks) - 2, -1, -1):
        out = jnp.where(s == k, blks[k], out)
    return out


def _proj_tok_kernel(m_ref, w_ref, wg_ref, g_ref, b_ref, o_ref, wb_ref, wgb_ref):
    s = pl.program_id(0)

    @pl.when(pl.program_id(1) == 0)
    def _():
        wb_ref[...] = w_ref[...].astype(BF16)
        wgb_ref[...] = wg_ref[...].astype(BF16)

    m = m_ref[...]
    acc = jnp.dot(m, wb_ref[...], preferred_element_type=F32)

    @pl.when(s == _S_H)
    def _():
        o_ref[0] = (acc * _sigmoid(jnp.dot(m, wgb_ref[...], preferred_element_type=F32))).astype(BF16)

    @pl.when(s == _S_U)
    def _():
        o_ref[0] = acc.astype(BF16)

    @pl.when(s == _S_V)
    def _():
        o_ref[0] = _layernorm(acc, g_ref[...], b_ref[...]).astype(BF16)

    @pl.when(s >= _S_GA)
    def _():
        o_ref[0] = _sigmoid(acc).astype(BF16)


def _proj_tok(geom, m, w_in, sgu_g, sgu_b, layer, tm):
    d = m.shape[-1]
    n_slab = len(_TOK_BLKS)
    return pl.pallas_call(
        _proj_tok_kernel,
        grid=(n_slab, geom.t // tm),
        in_specs=[
            pl.BlockSpec((tm, d), lambda s, i: (i, 0)),
            pl.BlockSpec((None, d, d), lambda s, i: (layer, 0, _blk_lookup(_TOK_BLKS, s))),
            pl.BlockSpec((None, d, d), lambda s, i: (layer, 0, _W_GATE)),
            _layer_spec((1, d), layer), _layer_spec((1, d), layer),
        ],
        out_specs=pl.BlockSpec((1, tm, d), lambda s, i: (s, i, 0)),
        out_shape=jax.ShapeDtypeStruct((n_slab, geom.t, d), BF16),
        scratch_shapes=[pltpu.VMEM((d, d), BF16), pltpu.VMEM((d, d), BF16)],
        compiler_params=_cparams("arbitrary", "arbitrary"),
        name=f"proj_tok{layer}",
    )(m, w_in, w_in, sgu_g, sgu_b)


def _proj_head_kernel(m_ref, w_ref, lb_ref, o_ref, wb_ref, *, layer, n_heads, forget):
    k = pl.program_id(0)

    @pl.when(pl.program_id(1) == 0)
    def _():
        wb_ref[...] = w_ref[...].astype(BF16)

    acc = jnp.dot(m_ref[...], wb_ref[...], preferred_element_type=F32)

    def store(y):
        for h in range(n_heads):
            o_ref[0, h] = y[:, h * LANES:(h + 1) * LANES].astype(o_ref.dtype)

    if forget:
        lb_all = lb_ref[...]
        e = jnp.exp(lb_all - jnp.max(lb_all, axis=0, keepdims=True))
        soft = e / jnp.sum(e, axis=0, keepdims=True)
        lb = jnp.sum(soft[:layer + 1], axis=0, keepdims=True) - soft[0:1]
        store(lb + (1.0 - lb) * _sigmoid(acc))
    else:
        @pl.when(k != _K_SG)
        def _():
            store(acc)

        @pl.when(k == _K_SG)
        def _():
            store(_silu(acc))


def _proj_head(geom, m, w_in, hgrn_lb, layer, tm, *, forget):
    d = m.shape[-1]
    n_heads = d // LANES
    blks = _HF_BLKS if forget else _HB_BLKS
    dtype = F32 if forget else BF16
    return pl.pallas_call(
        functools.partial(_proj_head_kernel, layer=layer, n_heads=n_heads, forget=forget),
        grid=(len(blks), geom.t // tm),
        in_specs=[
            pl.BlockSpec((tm, d), lambda k, i: (i, 0)),
            pl.BlockSpec((None, d, d), lambda k, i: (layer, 0, _blk_lookup(blks, k))),
            _const_spec(hgrn_lb.shape),
        ],
        out_specs=pl.BlockSpec((1, n_heads, tm, LANES), lambda k, i: (k, 0, i, 0)),
        out_shape=jax.ShapeDtypeStruct((len(blks), n_heads, geom.t, LANES), dtype),
        scratch_shapes=[pltpu.VMEM((d, d), BF16)],
        compiler_params=_cparams("arbitrary", "arbitrary"),
        name=f"proj_h{'f' if forget else 'b'}{layer}",
    )(m, w_in, hgrn_lb)


def _level_table():
    tok = _slab_tokens()
    t, s = tok[:, None], tok[None, :]
    x = t ^ s
    lvl = np.zeros((CHUNK, CHUNK), np.int32)
    for k in range(N_LEVELS):
        lvl = np.where((x >> k) > 0, k + 1, lvl)
    lvl = np.where(s > t, -1, lvl)
    return lvl.astype(np.int32)


def _row_bcast(x, r):
    return jnp.broadcast_to(x[r:r + 1, :], x.shape)


def _sublane_gather(src, rows_from, riota):
    out = jnp.ones_like(src)
    for r, rf in enumerate(rows_from):
        if rf is not None:
            out = jnp.where(riota == r, _row_bcast(src, rf), out)
    return out


def _hgrn_head(q, f, v_cat, st, lvl, *, reverse):
    kk_vec = [1.0 - fj for fj in f]
    riota = lax.broadcasted_iota(jnp.int32, (SUBLANES, LANES), 0)
    tgt = list(f)
    src = [jnp.ones_like(fj) for fj in f]

    def cat_bf16(slabs):
        return jnp.concatenate(slabs, axis=0).astype(BF16)

    def scores(qs, ks):
        return lax.dot_general(cat_bf16(qs), cat_bf16(ks), (((1,), (1,)), ((), ())),
                               preferred_element_type=F32)

    a = jnp.where(lvl == 0, scores(q, kk_vec), 0.0)
    for k in range(N_LEVELS):
        p = scores([q[j] * tgt[j] for j in range(SLABS)], [kk_vec[j] * src[j] for j in range(SLABS)])
        a = jnp.where(lvl == k + 1, p, a)
        if k < 4:
            bit, low = 1 << k, (1 << k) - 1
            new_tgt, new_src = list(tgt), list(src)
            for j in range(SLABS):
                in_far_half = bool(j & bit) != reverse
                if in_far_half:
                    sib_edge = ((j & ~bit) | low) if not reverse else ((j | bit) & ~low)
                    new_tgt[j] = tgt[j] * tgt[sib_edge]
                else:
                    sib_edge = ((j | bit) | low) if not reverse else ((j & ~bit) & ~low)
                    new_src[j] = src[j] * tgt[sib_edge]
            tgt, src = new_tgt, new_src
        else:
            bit, low = 1 << (k - 4), (1 << (k - 4)) - 1
            edge = tgt[SLABS - 1] if not reverse else tgt[0]
            rows_t, rows_s = [], []
            for r in range(SUBLANES):
                in_far_half = bool(r & bit) != reverse
                if in_far_half:
                    rows_t.append(((r & ~bit) | low) if not reverse else ((r | bit) & ~low))
                    rows_s.append(None)
                else:
                    rows_t.append(None)
                    rows_s.append(((r | bit) | low) if not reverse else ((r & ~bit) & ~low))
            fac_t = _sublane_gather(edge, rows_t, riota)
            fac_s = _sublane_gather(edge, rows_s, riota)
            tgt = [x * fac_t for x in tgt]
            src = [x * fac_s for x in src]

    o = jnp.dot(a.astype(BF16), v_cat, preferred_element_type=F32)
    o += lax.dot_general(cat_bf16([q[j] * tgt[j] for j in range(SLABS)]), st.astype(BF16),
                         (((1,), (1,)), ((), ())), preferred_element_type=F32)
    total = tgt[SLABS - 1][SUBLANES - 1:SUBLANES, :] if not reverse else tgt[0][0:1, :]
    upd = lax.dot_general(v_cat, cat_bf16([kk_vec[j] * src[j] for j in range(SLABS)]),
                          (((0,), (0,)), ((), ())), preferred_element_type=F32)
    return o, st * total + upd


def _hgrn_kernel(*refs, geom, reverse, n_heads):
    if reverse:
        q_ref, v_ref, sg_ref, f_ref, of_ref, ng_ref, lvl_ref, s0_ref, o_ref, sf_ref, st_ref = refs
    else:
        q_ref, v_ref, f_ref, lvl_ref, s0_ref, o_ref, sf_ref, st_ref = refs
    c = pl.program_id(0)
    n_chunks = geom.t // CHUNK
    ce = (n_chunks - 1 - c) if reverse else c
    row0 = ce * CHUNK
    pos = geom.pos_in_seq(row0)
    is_first = (pos == geom.seq_len(row0) - CHUNK) if reverse else (pos == 0)
    is_last = (pos == 0) if reverse else (pos == geom.seq_len(row0) - CHUNK)

    @pl.when(is_first)
    def _():
        for h in range(n_heads):
            st_ref[h] = s0_ref[0, h].T

    lvl = lvl_ref[...]
    for h in range(n_heads):
        q32 = q_ref[0, h].astype(F32)
        f32 = f_ref[0, h]
        slab = lambda x: [x[j * SUBLANES:(j + 1) * SUBLANES, :] for j in range(SLABS)]
        o, st_new = _hgrn_head(slab(q32), slab(f32), v_ref[0, h], st_ref[h], lvl, reverse=reverse)
        st_ref[h] = st_new
        if reverse:
            o = o + of_ref[h]
            o = o * lax.rsqrt(jnp.mean(o * o, axis=-1, keepdims=True) + LN_EPS)
            cols = pl.ds(h * LANES, LANES)
            o_ref[:, cols] = (o * ng_ref[:, cols] * sg_ref[0, h].astype(F32)).astype(BF16)
        else:
            o_ref[h] = o

    @pl.when(is_last)
    def _():
        for h in range(n_heads):
            sf_ref[0, h] = st_ref[h].T


def _hgrn(geom, proj_hb, proj_hf, s0, lvl, *, reverse, layer, o_fwd=None, norm_g=None):
    _, n_heads, t, _ = proj_hb.shape
    d = n_heads * LANES
    n_chunks = t // CHUNK

    def chunk(c):
        return (n_chunks - 1 - c) if reverse else c

    def blk(kind):
        return pl.BlockSpec((1, n_heads, CHUNK, LANES), lambda c: (kind, 0, chunk(c), 0))

    head_chunk = pl.BlockSpec((n_heads, CHUNK, LANES), lambda c: (0, chunk(c), 0))
    state_spec = pl.BlockSpec((1, n_heads, LANES, LANES), lambda c: (geom.seq_of(chunk(c) * CHUNK), 0, 0, 0))
    lvl_spec = _const_spec((CHUNK, CHUNK))
    state_shape = jax.ShapeDtypeStruct((geom.n_seq, n_heads, LANES, LANES), F32)
    if reverse:
        in_specs = [blk(_K_Q), blk(_K_I), blk(_K_SG), blk(1), head_chunk, _layer_spec((1, d), layer),
                    lvl_spec, state_spec]
        args = (proj_hb, proj_hb, proj_hb, proj_hf, o_fwd, norm_g, lvl, s0)
        out_specs = [pl.BlockSpec((CHUNK, d), lambda c: (chunk(c), 0)), state_spec]
        out_shape = [jax.ShapeDtypeStruct((t, d), BF16), state_shape]
    else:
        in_specs = [blk(_K_Q), blk(_K_I), blk(0), lvl_spec, state_spec]
        args = (proj_hb, proj_hb, proj_hf, lvl, s0)
        out_specs = [head_chunk, state_spec]
        out_shape = [jax.ShapeDtypeStruct((n_heads, t, LANES), F32), state_shape]
    return pl.pallas_call(
        functools.partial(_hgrn_kernel, geom=geom, reverse=reverse, n_heads=n_heads),
        grid=(n_chunks,),
        in_specs=in_specs,
        out_specs=out_specs,
        out_shape=out_shape,
        scratch_shapes=[pltpu.VMEM((n_heads, LANES, LANES), F32)],
        compiler_params=_cparams("arbitrary"),
        name=f"hgrn{layer}{'b' if reverse else 'f'}",
    )(*args)


def _conv_slab_order(cur, prev_edge, next_edge, w_ref, bias, lanes, n_chunks, conv_k):
    half = conv_k // 2
    riota = lax.broadcasted_iota(jnp.int32, (SUBLANES, LANES), 0)
    slab = lambda c, j: cur[c * CHUNK + j * SUBLANES:c * CHUNK + (j + 1) * SUBLANES, :]
    w = [w_ref[k:k + 1, lanes] for k in range(conv_k)]
    outs = []
    for c in range(n_chunks):
        x = [slab(c, j) for j in range(SLABS)]
        up, down = [], []
        for j in range(SLABS):
            nxt = next_edge[j] if c == n_chunks - 1 else slab(c + 1, j)[0:1, :]
            prv = prev_edge[j] if c == 0 else slab(c - 1, j)[SUBLANES - 1:SUBLANES, :]
            up.append(jnp.where(riota == SUBLANES - 1, nxt, pltpu.roll(x[j], SUBLANES - 1, 0)))
            down.append(jnp.where(riota == 0, prv, pltpu.roll(x[j], 1, 0)))
        for j in range(SLABS):
            acc = bias
            for k in range(conv_k):
                s = j + k - half
                term = x[s] if 0 <= s < SLABS else (up[s - SLABS] if s >= SLABS else down[s + SLABS])
                acc = acc + term * w[k]
            outs.append(acc)
    return jnp.concatenate(outs, axis=0)


def _mix_kernel(h_ref, hp_ref, hn_ref, u_ref, vn_ref, ga_ref, gb_ref, gc_ref, ob_ref, x_ref, mod_ref,
                cw_ref, cb_ref, clg_ref, clb_ref, wa_ref, wb_ref, sw_ref, sb_ref, wc_ref, wo_ref,
                l1g_ref, l1b_ref,
                x1_ref, m2_ref, wab_ref, wbb_ref, wcb_ref, wob_ref, conv_ref,
                *, geom, tm, d, conv_k, alpha):
    i = pl.program_id(0)
    row0 = i * tm
    cid = geom.cond_row(row0)
    pos = geom.pos_in_seq(row0)
    has_prev = pos > 0
    has_next = pos + tm < geom.seq_len(row0)
    n_chunks = tm // CHUNK

    @pl.when(i == 0)
    def _():
        wab_ref[...] = wa_ref[...].astype(BF16)
        wbb_ref[...] = wb_ref[...].astype(BF16)
        wcb_ref[...] = wc_ref[...].astype(BF16)
        wob_ref[...] = wo_ref[...].astype(BF16)

    h = h_ref[0].astype(F32)
    hp = jnp.where(has_prev, hp_ref[0].astype(F32), 0.0)
    hn = jnp.where(has_next, hn_ref[0].astype(F32), 0.0)
    for lt in range(d // LANES):
        lanes = pl.ds(lt * LANES, LANES)
        cl = slice(lt * LANES, (lt + 1) * LANES)
        prev_edge = [hp[j * SUBLANES + SUBLANES - 1:(j + 1) * SUBLANES, cl] for j in range(SLABS)]
        next_edge = [hn[j * SUBLANES:j * SUBLANES + 1, cl] for j in range(SLABS)]
        conv_ref[:, lanes] = _conv_slab_order(h[:, cl], prev_edge, next_edge, cw_ref, cb_ref[:, lanes],
                                              lanes, n_chunks, conv_k)
    ha = _silu(_layernorm(conv_ref[...], clg_ref[...], clb_ref[...]))
    mix = ga_ref[0].astype(F32) * jnp.dot(ha.astype(BF16), wab_ref[...], preferred_element_type=F32)

    mix += gb_ref[0].astype(F32) * jnp.dot(ob_ref[...], wbb_ref[...], preferred_element_type=F32)

    vn = vn_ref[0]
    rows = []
    for ch in range(n_chunks):
        cols = []
        for g in range(d // LANES):
            vg = vn[ch * CHUNK:(ch + 1) * CHUNK, g * LANES:(g + 1) * LANES]
            cols.append(jnp.dot(sw_ref[g], vg, preferred_element_type=F32) + sb_ref[g])
        rows.append(jnp.concatenate(cols, axis=-1))
    mixed = jnp.concatenate(rows, axis=0)
    yc = jnp.dot((u_ref[0].astype(F32) * mixed).astype(BF16), wcb_ref[...], preferred_element_type=F32)
    mix += gc_ref[0].astype(F32) * yc

    y = jnp.dot(mix.astype(BF16), wob_ref[...], preferred_element_type=F32)
    g1, sh2, sc2 = _mod_rows(mod_ref, cid, d, (2, 3, 4))
    x1 = _layernorm(alpha * x_ref[...] + g1 * y, l1g_ref[...], l1b_ref[...])
    x1_ref[...] = x1
    m2_ref[...] = (x1 * (1.0 + sc2) + sh2).astype(BF16)


def _mix(geom, proj, ob, x, mod, p, sgu_w, sgu_b, layer, tm, alpha):
    _, t, d = proj.shape
    conv_k = p["conv_w"].shape[1]
    n_g = sgu_w.shape[0]
    assert conv_k // 2 < SLABS and tm % CHUNK == 0 and n_g * LANES == d
    r = tm // CHUNK
    n_chunks = t // CHUNK

    def blk(s):
        return pl.BlockSpec((1, tm, d), lambda i: (s, i, 0))

    tok = pl.BlockSpec((tm, d), lambda i: (i, 0))
    vec = _layer_spec((1, d), layer)
    mat = _layer_spec((d, d), layer)
    in_specs = [blk(_S_H),
                pl.BlockSpec((1, CHUNK, d), lambda i: (_S_H, jnp.maximum(i * r - 1, 0), 0)),
                pl.BlockSpec((1, CHUNK, d), lambda i: (_S_H, jnp.minimum((i + 1) * r, n_chunks - 1), 0)),
                blk(_S_U), blk(_S_V), blk(_S_GA), blk(_S_GB), blk(_S_GC), tok, tok,
                _layer_spec(mod.shape[1:], layer),
                _layer_spec((conv_k, d), layer), vec, vec, vec, mat, mat,
                _const_spec((n_g, CHUNK, CHUNK)), _const_spec((n_g, CHUNK, 1)), mat, mat, vec, vec]
    return pl.pallas_call(
        functools.partial(_mix_kernel, geom=geom, tm=tm, d=d, conv_k=conv_k, alpha=alpha),
        grid=(t // tm,),
        in_specs=in_specs,
        out_specs=[tok, tok],
        out_shape=[jax.ShapeDtypeStruct((t, d), F32), jax.ShapeDtypeStruct((t, d), BF16)],
        scratch_shapes=[pltpu.VMEM((d, d), BF16)] * 4 + [pltpu.VMEM((tm, d), F32)],
        compiler_params=_cparams("arbitrary"),
        name=f"mix{layer}",
    )(proj, proj, proj, proj, proj, proj, proj, proj, ob, x, mod,
      p["conv_w"], p["conv_b"], p["conv_ln_g"], p["conv_ln_b"], p["w_a_out"], p["w_b_out"],
      sgu_w, sgu_b, p["w_c_out"], p["w_o"], p["ln1_g"], p["ln1_b"])


def _ffn_kernel(m2_ref, x1_ref, mod_ref, modn_ref, wi_ref, wo_ref, g_ref, b_ref, x2_ref, *mn_ref,
                geom, tm, d, f_hidden, f_chunk, alpha):
    i = pl.program_id(0)
    cid = geom.cond_row(i * tm)
    m2 = m2_ref[...]
    acc = jnp.zeros((tm, d), F32)
    for c0 in range(0, f_hidden, f_chunk):
        gte = jnp.dot(m2, wi_ref[:, pl.ds(c0, f_chunk)], preferred_element_type=F32)
        up = jnp.dot(m2, wi_ref[:, pl.ds(f_hidden + c0, f_chunk)], preferred_element_type=F32)
        acc += jnp.dot((_silu(gte) * up).astype(BF16), wo_ref[pl.ds(c0, f_chunk), :],
                       preferred_element_type=F32)
    (g2,) = _mod_rows(mod_ref, cid, d, (5,))
    x2 = _layernorm(alpha * x1_ref[...] + g2 * acc, g_ref[...], b_ref[...])
    x2_ref[...] = x2
    if mn_ref:
        sh1, sc1 = _mod_rows(modn_ref, cid, d, (0, 1))
        mn_ref[0][...] = (x2 * (1.0 + sc1) + sh1).astype(BF16)


def _ffn(geom, m2, x1, mod, w_ffn_in_bf16, w_ffn_out_bf16, ln2_g, ln2_b, layer, tm, alpha):
    t, d = x1.shape
    n_layers, f_hidden, _ = w_ffn_out_bf16.shape
    f_chunk = f_hidden // 2
    assert f_chunk % LANES == 0
    emit_next = layer + 1 < n_layers
    tok = pl.BlockSpec((tm, d), lambda i: (i, 0))
    out_shape = [jax.ShapeDtypeStruct((t, d), F32)] + ([jax.ShapeDtypeStruct((t, d), BF16)] if emit_next else [])
    return pl.pallas_call(
        functools.partial(_ffn_kernel, geom=geom, tm=tm, d=d, f_hidden=f_hidden, f_chunk=f_chunk, alpha=alpha),
        grid=(t // tm,),
        in_specs=[tok, tok, _layer_spec(mod.shape[1:], layer),
                  _layer_spec(mod.shape[1:], layer + 1 if emit_next else layer),
                  _layer_spec((d, 2 * f_hidden), layer), _layer_spec((f_hidden, d), layer),
                  _layer_spec((1, d), layer), _layer_spec((1, d), layer)],
        out_specs=[tok] * len(out_shape),
        out_shape=out_shape,
        compiler_params=_cparams("arbitrary"),
        name=f"ffn{layer}",
    )(m2, x1, mod, mod, w_ffn_in_bf16, w_ffn_out_bf16, ln2_g, ln2_b)


def _sincos_2d(n_tokens, dim):
    rows = n_tokens // GRID_W
    t = jnp.arange(rows * GRID_W)
    r = (t // GRID_W).astype(F32)
    col = (t % GRID_W).astype(F32)
    nf = dim // 4
    omega = 1.0 / (10000.0 ** (jnp.arange(nf, dtype=F32) / nf))
    ar = r[:, None] * omega
    ac = col[:, None] * omega
    return jnp.concatenate([jnp.sin(ar), jnp.cos(ar), jnp.sin(ac), jnp.cos(ac)], axis=-1)


def kernel(x_prompt, x_sample, state_hgrn, c, c_ctx, ln_in_g, ln_in_b, w_ada, b_ada, w_in, conv_w, conv_b, conv_ln_g, conv_ln_b, w_a_out, hgrn_lb, hgrn_norm_g, w_b_out, sgu_ln_g, sgu_ln_b, sgu_w, sgu_b, w_c_out, w_o, ln1_g, ln1_b, w_ffn_in, w_ffn_out, ln2_g, ln2_b):
    n_ctx_seq, ctx_len, d = x_prompt.shape
    n_lat_seq, lat_len, _ = x_sample.shape
    n_layers = w_in.shape[0]
    n_heads = state_hgrn.shape[3]
    geom = _Geom(n_ctx_seq, ctx_len, n_lat_seq, lat_len)
    alpha = float((2 * n_layers) ** 0.25)
    assert d == n_heads * LANES and w_in.shape[2] == N_BLK * d
    assert 1 + n_lat_seq <= COND_ROWS and ctx_len % CHUNK == 0 and lat_len % CHUNK == 0

    def row_tile(limit, within_seq):
        tm = CHUNK
        while (2 * tm <= limit and lat_len % (2 * tm) == 0
               and (ctx_len if within_seq else geom.t_ctx) % (2 * tm) == 0):
            tm *= 2
        return tm

    tm_mix = row_tile(256, True)
    tm_ffn = row_tile(512, False)
    tm_proj = row_tile(1024, False)

    cond = jnp.zeros((COND_ROWS, d), F32).at[0].set(c_ctx).at[1:1 + n_lat_seq].set(c)
    mod = _ada(cond, w_ada, b_ada)

    vec = lambda a: a.reshape(n_layers, 1, d)
    tok = _slab_tokens()
    pos = _to_slab_order(_sincos_2d(lat_len, d))
    x, m = _pre(geom, _to_slab_order(x_prompt.reshape(geom.t_ctx, d)), _to_slab_order(x_sample.reshape(geom.t_lat, d)),
                pos, ln_in_g.reshape(1, d), ln_in_b.reshape(1, d), mod, tm_ffn)

    lvl_f = jnp.asarray(_level_table())
    lvl_b = jnp.asarray(_level_table().T.copy())
    zeros_ctx = jnp.zeros((n_ctx_seq, n_heads, LANES, LANES), F32)
    p = dict(conv_w=conv_w, conv_b=vec(conv_b), conv_ln_g=vec(conv_ln_g), conv_ln_b=vec(conv_ln_b),
             w_a_out=w_a_out, w_b_out=w_b_out, w_c_out=w_c_out, w_o=w_o, ln1_g=vec(ln1_g), ln1_b=vec(ln1_b))
    norm_g, sgu_g, sgu_bb = vec(hgrn_norm_g), vec(sgu_ln_g), vec(sgu_ln_b)
    ln2_gv, ln2_bv = vec(ln2_g), vec(ln2_b)
    w_ffn_in_bf16, w_ffn_out_bf16 = w_ffn_in.astype(BF16), w_ffn_out.astype(BF16)
    states = []
    for l in range(n_layers):
        proj = _proj_tok(geom, m, w_in, sgu_g, sgu_bb, l, tm_proj)
        proj_hb = _proj_head(geom, m, w_in, hgrn_lb, l, tm_proj, forget=False)
        proj_hf = _proj_head(geom, m, w_in, hgrn_lb, l, tm_proj, forget=True)
        s0_f = jnp.concatenate([zeros_ctx, state_hgrn[:, l, 0]], axis=0)
        s0_b = jnp.concatenate([zeros_ctx, state_hgrn[:, l, 1]], axis=0)
        o_f, sf_f = _hgrn(geom, proj_hb, proj_hf, s0_f, lvl_f, reverse=False, layer=l)
        ob, sf_b = _hgrn(geom, proj_hb, proj_hf, s0_b, lvl_b, reverse=True, layer=l, o_fwd=o_f, norm_g=norm_g)
        states.append(jnp.stack([sf_f[:n_ctx_seq], sf_b[:n_ctx_seq]], axis=1))
        sw = sgu_w[l][:, tok][:, :, tok].astype(BF16)
        sb = sgu_b[l][:, tok][:, :, None]
        x1, m2 = _mix(geom, proj, ob, x, mod, p, sw, sb, l, tm_mix, alpha)
        out = _ffn(geom, m2, x1, mod, w_ffn_in_bf16, w_ffn_out_bf16, ln2_gv, ln2_bv, l, tm_ffn, alpha)
        x, m = (out[0], out[1]) if l + 1 < n_layers else (out[0], None)

    x = _from_slab_order(x)
    y_prompt = x[:geom.t_ctx].reshape(x_prompt.shape)
    y_sample = x[geom.t_ctx:].reshape(x_sample.shape)
    return y_prompt, y_sample, jnp.stack(states, axis=1)
```

```python
import functools

import jax
import jax.numpy as jnp
import numpy as np
from jax import lax
from jax.experimental import pallas as pl
from jax.experimental.pallas import tpu as pltpu

LN_EPS = 1e-5
GRID_W = 64
LANES = 128
SUBLANES = 8
CHUNK = 128
SLABS = CHUNK // SUBLANES
N_LEVELS = 7
COND_ROWS = 8
PROJ_SUB_ROWS = 256
VMEM_LIMIT = 56 * 1024 * 1024

F32 = jnp.float32
BF16 = jnp.bfloat16


def _cparams(*sem):
    return pltpu.CompilerParams(dimension_semantics=sem, vmem_limit_bytes=VMEM_LIMIT)


def _sigmoid(x):
    return 1.0 / (1.0 + jnp.exp(-x))


def _silu(x):
    return x * _sigmoid(x)


def _layernorm(x, g, b):
    mu = jnp.mean(x, axis=-1, keepdims=True)
    xc = x - mu
    var = jnp.mean(xc * xc, axis=-1, keepdims=True)
    return xc * lax.rsqrt(var + LN_EPS) * g + b


def _const_spec(shape):
    nd = len(shape)
    return pl.BlockSpec(shape, lambda *_: (0,) * nd, pipeline_mode=pl.Buffered(1))


def _layer_spec(shape, layer):
    nd = len(shape)
    return pl.BlockSpec((None,) + tuple(shape), lambda *_: (layer,) + (0,) * nd, pipeline_mode=pl.Buffered(1))


def _slab_tokens():
    p = np.arange(CHUNK)
    return SLABS * (p % SUBLANES) + p // SUBLANES


def _to_slab_order(x):
    t = x.shape[0]
    rest = x.shape[1:]
    return x.reshape((t // CHUNK, SUBLANES, SLABS) + rest).swapaxes(1, 2).reshape((t,) + rest)


def _from_slab_order(x):
    t = x.shape[0]
    rest = x.shape[1:]
    return x.reshape((t // CHUNK, SLABS, SUBLANES) + rest).swapaxes(1, 2).reshape((t,) + rest)


def _ada_kernel(cond_ref, w_ref, b_ref, o_ref):
    s = _silu(cond_ref[...]).astype(BF16)
    o_ref[0] = jnp.dot(s, w_ref[0].astype(BF16), preferred_element_type=F32) + b_ref[0]


def _ada(cond, w_ada, b_ada):
    n_layers, d, n6 = w_ada.shape
    tn = n6 // 4
    return pl.pallas_call(
        _ada_kernel,
        grid=(n_layers, n6 // tn),
        in_specs=[
            pl.BlockSpec((COND_ROWS, d), lambda l, j: (0, 0)),
            pl.BlockSpec((1, d, tn), lambda l, j: (l, 0, j)),
            pl.BlockSpec((1, 1, tn), lambda l, j: (l, 0, j)),
        ],
        out_specs=pl.BlockSpec((1, COND_ROWS, tn), lambda l, j: (l, 0, j)),
        out_shape=jax.ShapeDtypeStruct((n_layers, COND_ROWS, n6), F32),
        compiler_params=_cparams("arbitrary", "arbitrary"),
        name="ada",
    )(cond, w_ada, b_ada.reshape(n_layers, 1, n6))


class _Geom:
    def __init__(self, n_ctx_seq, ctx_len, n_lat_seq, lat_len):
        self.n_ctx_seq, self.ctx_len = n_ctx_seq, ctx_len
        self.n_lat_seq, self.lat_len = n_lat_seq, lat_len
        self.t_ctx = n_ctx_seq * ctx_len
        self.t_lat = n_lat_seq * lat_len
        self.t = self.t_ctx + self.t_lat
        self.n_seq = n_ctx_seq + n_lat_seq

    def cond_row(self, row0):
        return jnp.where(row0 < self.t_ctx, 0, 1 + (row0 - self.t_ctx) // self.lat_len)

    def seq_of(self, row0):
        return jnp.where(row0 < self.t_ctx, row0 // self.ctx_len,
                         self.n_ctx_seq + (row0 - self.t_ctx) // self.lat_len)

    def pos_in_seq(self, row0):
        return jnp.where(row0 < self.t_ctx, row0 % self.ctx_len, (row0 - self.t_ctx) % self.lat_len)

    def seq_len(self, row0):
        return jnp.where(row0 < self.t_ctx, self.ctx_len, self.lat_len)


def _mod_rows(mod_ref, cid, d, idx):
    return [mod_ref[pl.ds(cid, 1), pl.ds(i * d, d)] for i in idx]


def _pre_kernel(xc_ref, xl_ref, pos_ref, g_ref, b_ref, mod_ref, x_ref, m_ref, *, geom, tm, d):
    i = pl.program_id(0)
    row0 = i * tm
    cid = geom.cond_row(row0)
    sh1, sc1 = _mod_rows(mod_ref, cid, d, (0, 1))

    def emit(x):
        xn = _layernorm(x, g_ref[...], b_ref[...])
        x_ref[...] = xn
        m_ref[...] = (xn * (1.0 + sc1) + sh1).astype(BF16)

    @pl.when(row0 < geom.t_ctx)
    def _():
        emit(xc_ref[...])

    @pl.when(row0 >= geom.t_ctx)
    def _():
        emit(xl_ref[...] + pos_ref[...])


def _pre(geom, x_ctx, x_lat, pos, g, b, mod, tm):
    d = x_ctx.shape[-1]
    n_ctx_tiles = geom.t_ctx // tm
    pos_tiles = geom.lat_len // tm
    return pl.pallas_call(
        functools.partial(_pre_kernel, geom=geom, tm=tm, d=d),
        grid=(geom.t // tm,),
        in_specs=[
            pl.BlockSpec((tm, d), lambda i: (jnp.minimum(i, n_ctx_tiles - 1), 0)),
            pl.BlockSpec((tm, d), lambda i: (jnp.maximum(i - n_ctx_tiles, 0), 0)),
            pl.BlockSpec((tm, d), lambda i: (jnp.maximum(i - n_ctx_tiles, 0) % pos_tiles, 0)),
            _const_spec((1, d)), _const_spec((1, d)), _layer_spec(mod.shape[1:], 0),
        ],
        out_specs=[pl.BlockSpec((tm, d), lambda i: (i, 0))] * 2,
        out_shape=[jax.ShapeDtypeStruct((geom.t, d), F32), jax.ShapeDtypeStruct((geom.t, d), BF16)],
        compiler_params=_cparams("arbitrary"),
        name="pre",
    )(x_ctx, x_lat, pos, g, b, mod)


_W_VAL, _W_GATE, _W_Q, _W_ZF, _W_ZB, _W_I, _W_G, _W_U, _W_V, _W_GA, _W_GB, _W_GC = range(12)
N_BLK = 12
_PROJ_MODES = {
    "id": ((_W_U, _W_Q, _W_I), 0),
    "sig": ((_W_GA, _W_GB, _W_GC), 3),
    "silu": ((_W_G,), 6),
    "forget": ((_W_ZF, _W_ZB), 7),
}
_ID_U, _ID_Q, _ID_I = range(3)
_F_FWD, _F_BWD = range(2)


def _blk_lookup(blks, s):
    out = blks[-1]
    for k in range(len(blks) - 2, -1, -1):
        out = jnp.where(s == k, blks[k], out)
    return out


def _proj_a_kernel(m_ref, wv_ref, wg_ref, ws_ref, g_ref, b_ref, h_ref, vn_ref, wvb_ref, wgb_ref, wsb_ref):
    @pl.when(pl.program_id(0) == 0)
    def _():
        wvb_ref[...] = wv_ref[...].astype(BF16)
        wgb_ref[...] = wg_ref[...].astype(BF16)
        wsb_ref[...] = ws_ref[...].astype(BF16)

    for r0 in range(0, m_ref.shape[0], PROJ_SUB_ROWS):
        rs = pl.ds(r0, PROJ_SUB_ROWS)
        m = m_ref[rs, :]
        val = jnp.dot(m, wvb_ref[...], preferred_element_type=F32)
        gate = jnp.dot(m, wgb_ref[...], preferred_element_type=F32)
        h_ref[rs, :] = (val * _sigmoid(gate)).astype(BF16)
        v = jnp.dot(m, wsb_ref[...], preferred_element_type=F32)
        vn_ref[rs, :] = _layernorm(v, g_ref[...], b_ref[...]).astype(BF16)


def _proj_a(geom, m, w_in, sgu_g, sgu_b, layer, tm):
    d = m.shape[-1]
    tok = pl.BlockSpec((tm, d), lambda i: (i, 0))

    def wblk(b):
        return pl.BlockSpec((None, d, d), lambda i: (layer, 0, b), pipeline_mode=pl.Buffered(1))

    return pl.pallas_call(
        _proj_a_kernel,
        grid=(geom.t // tm,),
        in_specs=[tok, wblk(_W_VAL), wblk(_W_GATE), wblk(_W_V), _layer_spec((1, d), layer), _layer_spec((1, d), layer)],
        out_specs=[tok, tok],
        out_shape=[jax.ShapeDtypeStruct((geom.t, d), BF16)] * 2,
        scratch_shapes=[pltpu.VMEM((d, d), BF16)] * 3,
        compiler_params=_cparams("arbitrary"),
        name=f"proj_a{layer}",
    )(m, w_in, w_in, w_in, sgu_g, sgu_b)


def _load_slabs(ref, c):
    out = []
    for p in range(SLABS // 2):
        x = ref[pl.ds(c * CHUNK + 2 * p * SUBLANES, 2 * SUBLANES), :].astype(F32)
        out += [x[:SUBLANES], x[SUBLANES:]]
    return out


def _conv_chunk(x, prev, nxt, w_ref, bias, has_prev, has_next):
    conv_k = w_ref.shape[0]
    half = conv_k // 2
    riota = lax.broadcasted_iota(jnp.int32, (SUBLANES, LANES), 0)
    w = [w_ref[k:k + 1, :] for k in range(conv_k)]
    up, down = [], []
    for j in range(SLABS):
        first_next = jnp.where(has_next, nxt[j][0:1, :], 0.0)
        last_prev = jnp.where(has_prev, prev[j][SUBLANES - 1:SUBLANES, :], 0.0)
        up.append(jnp.where(riota == SUBLANES - 1, first_next, pltpu.roll(x[j], SUBLANES - 1, 0)))
        down.append(jnp.where(riota == 0, last_prev, pltpu.roll(x[j], 1, 0)))
    outs = []
    for j in range(SLABS):
        acc = bias
        for k in range(conv_k):
            s = j + k - half
            term = x[s] if 0 <= s < SLABS else (up[s - SLABS] if s >= SLABS else down[s + SLABS])
            acc = acc + term * w[k]
        outs.append(acc)
    return outs


def _proj_p_kernel(m_ref, w_ref, lb_ref, h_ref, hp_ref, hn_ref, cw_ref, cb_ref, o_ref, conv_ref, wb_ref,
                   *, geom, layer, n_heads, mode, conv_rows, conv_split):
    k = pl.program_id(0)
    i = pl.program_id(1)

    @pl.when(i == 0)
    def _():
        wb_ref[...] = w_ref[...].astype(BF16)

    if mode == "forget":
        lb_all = lb_ref[...]
        e = jnp.exp(lb_all - jnp.max(lb_all, axis=0, keepdims=True))
        soft = e / jnp.sum(e, axis=0, keepdims=True)
        lb = jnp.sum(soft[:layer + 1], axis=0, keepdims=True) - soft[0:1]
    epilogue = {"id": lambda a: a, "sig": _sigmoid, "silu": _silu,
                "forget": lambda a: lb + (1.0 - lb) * _sigmoid(a)}[mode]

    tm = m_ref.shape[0]
    n_conv = conv_rows // CHUNK
    n_sb = min(tm // PROJ_SUB_ROWS, n_conv)
    sb_rows = tm // n_sb
    row0 = (i * conv_split + k) * conv_rows if conv_split > 1 else i * conv_rows
    for sb in range(n_sb):
        for c in range(sb * n_conv // n_sb, (sb + 1) * n_conv // n_sb):
            pos = geom.pos_in_seq(row0 + c * CHUNK)
            x = _load_slabs(h_ref, c)
            prev = _load_slabs(h_ref, c - 1) if c > 0 else _load_slabs(hp_ref, 0)
            nxt = _load_slabs(h_ref, c + 1) if c + 1 < n_conv else _load_slabs(hn_ref, 0)
            outs = _conv_chunk(x, prev, nxt, cw_ref, cb_ref[...], pos > 0,
                               pos + CHUNK < geom.seq_len(row0 + c * CHUNK))
            for j, o in enumerate(outs):
                conv_ref[pl.ds(c * CHUNK + j * SUBLANES, SUBLANES), :] = o

        rs = pl.ds(sb * sb_rows, sb_rows)
        y = epilogue(jnp.dot(m_ref[rs, :], wb_ref[...], preferred_element_type=F32))
        for h in range(n_heads):
            o_ref[0, h, rs, :] = y[:, h * LANES:(h + 1) * LANES].astype(o_ref.dtype)


def _proj_p(geom, m, w_in, hgrn_lb, h, conv_w, conv_b, layer, tm, mode):
    d = m.shape[-1]
    n_heads = d // LANES
    blks, first_col = _PROJ_MODES[mode]
    n_kind = len(blks)
    conv_k = conv_w.shape[1]
    assert conv_k // 2 < SLABS
    if mode == "forget":
        assert first_col == n_heads - 1
        conv_split, conv_cols, conv_rows = n_kind, 1, tm // n_kind
        lane_col = lambda k: first_col
        conv_blk = lambda k, i: (i * conv_split + k, first_col)
        conv_out_blk = lambda k, i: (i * conv_split + k, 0)
    else:
        conv_split, conv_cols, conv_rows = 1, n_kind, tm
        lane_col = lambda k: first_col + k
        conv_blk = lambda k, i: (i, first_col + k)
        conv_out_blk = lambda k, i: (i, k)
    cpr = conv_rows // CHUNK
    n_chunks = geom.t // CHUNK
    return pl.pallas_call(
        functools.partial(_proj_p_kernel, geom=geom, layer=layer, n_heads=n_heads, mode=mode,
                          conv_rows=conv_rows, conv_split=conv_split),
        grid=(n_kind, geom.t // tm),
        in_specs=[
            pl.BlockSpec((tm, d), lambda k, i: (i, 0)),
            pl.BlockSpec((None, d, d), lambda k, i: (layer, 0, _blk_lookup(blks, k))),
            _const_spec(hgrn_lb.shape),
            pl.BlockSpec((conv_rows, LANES), conv_blk),
            pl.BlockSpec((CHUNK, LANES), lambda k, i: (jnp.maximum(conv_blk(k, i)[0] * cpr - 1, 0), lane_col(k))),
            pl.BlockSpec((CHUNK, LANES),
                         lambda k, i: (jnp.minimum((conv_blk(k, i)[0] + 1) * cpr, n_chunks - 1), lane_col(k))),
            pl.BlockSpec((None, conv_k, LANES), lambda k, i: (layer, 0, lane_col(k))),
            pl.BlockSpec((None, 1, LANES), lambda k, i: (layer, 0, lane_col(k))),
        ],
        out_specs=[pl.BlockSpec((1, n_heads, tm, LANES), lambda k, i: (k, 0, i, 0)),
                   pl.BlockSpec((conv_rows, LANES), conv_out_blk)],
        out_shape=[jax.ShapeDtypeStruct((n_kind, n_heads, geom.t, LANES), F32 if mode == "forget" else BF16),
                   jax.ShapeDtypeStruct((geom.t, conv_cols * LANES), F32)],
        scratch_shapes=[pltpu.VMEM((d, d), BF16)],
        compiler_params=_cparams("arbitrary", "arbitrary"),
        name=f"proj_{mode}{layer}",
    )(m, w_in, hgrn_lb, h, h, h, conv_w, conv_b)


def _level_table():
    tok = _slab_tokens()
    t, s = tok[:, None], tok[None, :]
    x = t ^ s
    lvl = np.zeros((CHUNK, CHUNK), np.int32)
    for k in range(N_LEVELS):
        lvl = np.where((x >> k) > 0, k + 1, lvl)
    lvl = np.where(s > t, -1, lvl)
    return lvl.astype(np.int32)


def _row_bcast(x, r):
    return jnp.broadcast_to(x[r:r + 1, :], x.shape)


def _sublane_gather(src, rows_from, riota):
    out = jnp.ones_like(src)
    for r, rf in enumerate(rows_from):
        if rf is not None:
            out = jnp.where(riota == r, _row_bcast(src, rf), out)
    return out


def _hgrn_head(q, f, v_cat, st, lvl, *, reverse):
    kk_vec = [1.0 - fj for fj in f]
    riota = lax.broadcasted_iota(jnp.int32, (SUBLANES, LANES), 0)
    tgt = list(f)
    src = [jnp.ones_like(fj) for fj in f]

    def cat_bf16(slabs):
        return jnp.concatenate(slabs, axis=0).astype(BF16)

    diag = jnp.sum(jnp.concatenate([q[j] * kk_vec[j] for j in range(SLABS)], axis=0), axis=-1, keepdims=True)
    a = jnp.where(lvl == 0, diag, 0.0)
    for k in range(N_LEVELS):
        p = lax.dot_general(cat_bf16([q[j] * tgt[j] for j in range(SLABS)]),
                            cat_bf16([kk_vec[j] * src[j] for j in range(SLABS)]),
                            (((1,), (1,)), ((), ())), preferred_element_type=F32)
        a = jnp.where(lvl == k + 1, p, a)
        if k < 4:
            bit, low = 1 << k, (1 << k) - 1
            new_tgt, new_src = list(tgt), list(src)
            for j in range(SLABS):
                in_far_half = bool(j & bit) != reverse
                if in_far_half:
                    sib_edge = ((j & ~bit) | low) if not reverse else ((j | bit) & ~low)
                    new_tgt[j] = tgt[j] * tgt[sib_edge]
                else:
                    sib_edge = ((j | bit) | low) if not reverse else ((j & ~bit) & ~low)
                    new_src[j] = src[j] * tgt[sib_edge]
            tgt, src = new_tgt, new_src
        else:
            bit, low = 1 << (k - 4), (1 << (k - 4)) - 1
            edge = tgt[SLABS - 1] if not reverse else tgt[0]
            rows_t, rows_s = [], []
            for r in range(SUBLANES):
                in_far_half = bool(r & bit) != reverse
                if in_far_half:
                    rows_t.append(((r & ~bit) | low) if not reverse else ((r | bit) & ~low))
                    rows_s.append(None)
                else:
                    rows_t.append(None)
                    rows_s.append(((r | bit) | low) if not reverse else ((r & ~bit) & ~low))
            fac_t = _sublane_gather(edge, rows_t, riota)
            fac_s = _sublane_gather(edge, rows_s, riota)
            tgt = [x * fac_t for x in tgt]
            src = [x * fac_s for x in src]

    lhs = jnp.concatenate([a.astype(BF16), cat_bf16([q[j] * tgt[j] for j in range(SLABS)])], axis=1)
    rhs = jnp.concatenate([v_cat, st.astype(BF16)], axis=0)
    o = jnp.dot(lhs, rhs, preferred_element_type=F32)
    edge = tgt[SLABS - 1] if not reverse else tgt[0]
    row = SUBLANES - 1 if not reverse else 0
    total = jnp.transpose(jnp.broadcast_to(edge[row:row + 1, :], (LANES, LANES)))
    upd = lax.dot_general(cat_bf16([kk_vec[j] * src[j] for j in range(SLABS)]), v_cat,
                          (((0,), (0,)), ((), ())), preferred_element_type=F32)
    return o, st * total + upd


def _hgrn_kernel(*refs, geom, reverse, n_heads, n_sub):
    if reverse:
        q_ref, v_ref, sg_ref, f_ref, of_ref, ng_ref, lvl_ref, s0_ref, o_ref, sf_ref, st_ref = refs
    else:
        q_ref, v_ref, f_ref, lvl_ref, s0_ref, o_ref, sf_ref, st_ref = refs
    rows = n_sub * CHUNK
    c = pl.program_id(0)
    n_steps = geom.t // rows
    ce = (n_steps - 1 - c) if reverse else c
    row0 = ce * rows
    pos = geom.pos_in_seq(row0)
    at_start = pos == 0
    at_end = pos == geom.seq_len(row0) - rows
    is_first, is_last = (at_end, at_start) if reverse else (at_start, at_end)

    @pl.when(is_first)
    def _():
        st_ref[...] = s0_ref[0]

    lvl = lvl_ref[...]
    slab = lambda x: [x[j * SUBLANES:(j + 1) * SUBLANES, :] for j in range(SLABS)]
    for cc in (reversed(range(n_sub)) if reverse else range(n_sub)):
        rs = pl.ds(cc * CHUNK, CHUNK)
        for h in range(n_heads):
            q32 = q_ref[0, h, rs, :].astype(F32)
            o, st_new = _hgrn_head(slab(q32), slab(f_ref[0, h, rs, :]), v_ref[0, h, rs, :], st_ref[h], lvl,
                                   reverse=reverse)
            st_ref[h] = st_new
            if reverse:
                o = o + of_ref[h, rs, :]
                o = o * lax.rsqrt(jnp.mean(o * o, axis=-1, keepdims=True) + LN_EPS)
                cols = pl.ds(h * LANES, LANES)
                o_ref[rs, cols] = (o * ng_ref[:, cols] * sg_ref[0, h, rs, :].astype(F32)).astype(BF16)
            else:
                o_ref[h, rs, :] = o

    @pl.when(is_last)
    def _():
        sf_ref[0] = st_ref[...]


def _hgrn(geom, p_id, p_silu, p_forget, s0, lvl, *, reverse, layer, n_sub, o_fwd=None, norm_g=None):
    _, n_heads, t, _ = p_id.shape
    d = n_heads * LANES
    rows = n_sub * CHUNK
    n_steps = t // rows

    def step(c):
        return (n_steps - 1 - c) if reverse else c

    def blk(kind):
        return pl.BlockSpec((1, n_heads, rows, LANES), lambda c: (kind, 0, step(c), 0))

    head_rows = pl.BlockSpec((n_heads, rows, LANES), lambda c: (0, step(c), 0))
    state_spec = pl.BlockSpec((1, n_heads, LANES, LANES), lambda c: (geom.seq_of(step(c) * rows), 0, 0, 0))
    lvl_spec = _const_spec((CHUNK, CHUNK))
    state_shape = jax.ShapeDtypeStruct((geom.n_seq, n_heads, LANES, LANES), F32)
    if reverse:
        in_specs = [blk(_ID_Q), blk(_ID_I), blk(0), blk(_F_BWD), head_rows, _layer_spec((1, d), layer),
                    lvl_spec, state_spec]
        args = (p_id, p_id, p_silu, p_forget, o_fwd, norm_g, lvl, s0)
        out_specs = [pl.BlockSpec((rows, d), lambda c: (step(c), 0)), state_spec]
        out_shape = [jax.ShapeDtypeStruct((t, d), BF16), state_shape]
    else:
        in_specs = [blk(_ID_Q), blk(_ID_I), blk(_F_FWD), lvl_spec, state_spec]
        args = (p_id, p_id, p_forget, lvl, s0)
        out_specs = [head_rows, state_spec]
        out_shape = [jax.ShapeDtypeStruct((n_heads, t, LANES), F32), state_shape]
    return pl.pallas_call(
        functools.partial(_hgrn_kernel, geom=geom, reverse=reverse, n_heads=n_heads, n_sub=n_sub),
        grid=(n_steps,),
        in_specs=in_specs,
        out_specs=out_specs,
        out_shape=out_shape,
        scratch_shapes=[pltpu.VMEM((n_heads, LANES, LANES), F32)],
        compiler_params=_cparams("arbitrary"),
        name=f"hgrn{layer}{'b' if reverse else 'f'}",
    )(*args)


def _mix_kernel(c0_ref, c1_ref, c2_ref, c3_ref, u_ref, vn_ref, ga_ref, gb_ref, gc_ref, ob_ref, x_ref, mod_ref,
                clg_ref, clb_ref, wa_ref, wb_ref, sw_ref, sb_ref, wc_ref, wo_ref, l1g_ref, l1b_ref,
                x1_ref, m2_ref, wab_ref, wbb_ref, wcb_ref, wob_ref, *, geom, tm, d, alpha):
    i = pl.program_id(0)
    cid = geom.cond_row(i * tm)
    n_heads = d // LANES

    @pl.when(i == 0)
    def _():
        wab_ref[...] = wa_ref[...].astype(BF16)
        wbb_ref[...] = wb_ref[...].astype(BF16)
        wcb_ref[...] = wc_ref[...].astype(BF16)
        wob_ref[...] = wo_ref[...].astype(BF16)

    def tokens(ref):
        return jnp.concatenate([ref[0, h] for h in range(n_heads)], axis=-1).astype(F32)

    mix = tokens(gb_ref) * jnp.dot(ob_ref[...], wbb_ref[...], preferred_element_type=F32)

    vn = vn_ref[...]
    rows = []
    for ch in range(tm // CHUNK):
        cols = []
        for g in range(n_heads):
            vg = vn[ch * CHUNK:(ch + 1) * CHUNK, g * LANES:(g + 1) * LANES]
            cols.append(jnp.dot(sw_ref[g], vg, preferred_element_type=F32) + sb_ref[g])
        rows.append(jnp.concatenate(cols, axis=-1))
    mixed = jnp.concatenate(rows, axis=0)
    yc = jnp.dot((tokens(u_ref) * mixed).astype(BF16), wcb_ref[...], preferred_element_type=F32)
    mix += tokens(gc_ref) * yc

    conv = jnp.concatenate([c0_ref[...], c1_ref[...], c2_ref[...], c3_ref[...]], axis=-1)
    ha = _silu(_layernorm(conv, clg_ref[...], clb_ref[...]))
    mix += tokens(ga_ref) * jnp.dot(ha.astype(BF16), wab_ref[...], preferred_element_type=F32)

    y = jnp.dot(mix.astype(BF16), wob_ref[...], preferred_element_type=F32)
    g1, sh2, sc2 = _mod_rows(mod_ref, cid, d, (2, 3, 4))
    x1 = _layernorm(alpha * x_ref[...] + g1 * y, l1g_ref[...], l1b_ref[...])
    x1_ref[...] = x1
    m2_ref[...] = (x1 * (1.0 + sc2) + sh2).astype(BF16)


def _mix(geom, convs, p_id, p_sig, vn, ob, x, mod, p, sgu_w, sgu_b, layer, tm, alpha):
    t, d = x.shape
    n_heads = d // LANES
    n_g = sgu_w.shape[0]
    assert tm % CHUNK == 0 and n_g * LANES == d and sum(c.shape[1] for c in convs) == d

    def kind(k):
        return pl.BlockSpec((1, n_heads, tm, LANES), lambda i: (k, 0, i, 0))

    tok = pl.BlockSpec((tm, d), lambda i: (i, 0))
    vec = _layer_spec((1, d), layer)
    mat = _layer_spec((d, d), layer)
    in_specs = [pl.BlockSpec((tm, c.shape[1]), lambda i: (i, 0)) for c in convs]
    in_specs += [kind(_ID_U), tok, kind(0), kind(1), kind(2), tok, tok,
                _layer_spec(mod.shape[1:], layer),
                vec, vec, mat, mat,
                _const_spec((n_g, CHUNK, CHUNK)), _const_spec((n_g, CHUNK, 1)), mat, mat, vec, vec]
    return pl.pallas_call(
        functools.partial(_mix_kernel, geom=geom, tm=tm, d=d, alpha=alpha),
        grid=(t // tm,),
        in_specs=in_specs,
        out_specs=[tok, tok],
        out_shape=[jax.ShapeDtypeStruct((t, d), F32), jax.ShapeDtypeStruct((t, d), BF16)],
        scratch_shapes=[pltpu.VMEM((d, d), BF16)] * 4,
        compiler_params=_cparams("arbitrary"),
        name=f"mix{layer}",
    )(*convs, p_id, vn, p_sig, p_sig, p_sig, ob, x, mod,
      p["conv_ln_g"], p["conv_ln_b"], p["w_a_out"], p["w_b_out"],
      sgu_w, sgu_b, p["w_c_out"], p["w_o"], p["ln1_g"], p["ln1_b"])


def _ffn_kernel(m2_ref, x1_ref, mod_ref, modn_ref, wi_ref, wo_ref, g_ref, b_ref, x2_ref, *mn_ref,
                geom, tm, d, f_hidden, f_chunk, alpha):
    i = pl.program_id(0)
    cid = geom.cond_row(i * tm)
    m2 = m2_ref[...]
    acc = jnp.zeros((tm, d), F32)
    for c0 in range(0, f_hidden, f_chunk):
        gte = jnp.dot(m2, wi_ref[:, pl.ds(c0, f_chunk)], preferred_element_type=F32)
        up = jnp.dot(m2, wi_ref[:, pl.ds(f_hidden + c0, f_chunk)], preferred_element_type=F32)
        acc += jnp.dot((_silu(gte) * up).astype(BF16), wo_ref[pl.ds(c0, f_chunk), :],
                       preferred_element_type=F32)
    (g2,) = _mod_rows(mod_ref, cid, d, (5,))
    x2 = _layernorm(alpha * x1_ref[...] + g2 * acc, g_ref[...], b_ref[...])
    x2_ref[...] = x2
    if mn_ref:
        sh1, sc1 = _mod_rows(modn_ref, cid, d, (0, 1))
        mn_ref[0][...] = (x2 * (1.0 + sc1) + sh1).astype(BF16)


def _ffn(geom, m2, x1, mod, w_ffn_in_bf16, w_ffn_out_bf16, ln2_g, ln2_b, layer, tm, alpha):
    t, d = x1.shape
    n_layers, f_hidden, _ = w_ffn_out_bf16.shape
    f_chunk = f_hidden // 2
    assert f_chunk % LANES == 0
    emit_next = layer + 1 < n_layers
    tok = pl.BlockSpec((tm, d), lambda i: (i, 0))
    out_shape = [jax.ShapeDtypeStruct((t, d), F32)] + ([jax.ShapeDtypeStruct((t, d), BF16)] if emit_next else [])
    return pl.pallas_call(
        functools.partial(_ffn_kernel, geom=geom, tm=tm, d=d, f_hidden=f_hidden, f_chunk=f_chunk, alpha=alpha),
        grid=(t // tm,),
        in_specs=[tok, tok, _layer_spec(mod.shape[1:], layer),
                  _layer_spec(mod.shape[1:], layer + 1 if emit_next else layer),
                  _layer_spec((d, 2 * f_hidden), layer), _layer_spec((f_hidden, d), layer),
                  _layer_spec((1, d), layer), _layer_spec((1, d), layer)],
        out_specs=[tok] * len(out_shape),
        out_shape=out_shape,
        compiler_params=_cparams("arbitrary"),
        name=f"ffn{layer}",
    )(m2, x1, mod, mod, w_ffn_in_bf16, w_ffn_out_bf16, ln2_g, ln2_b)


def _sincos_2d(n_tokens, dim):
    rows = n_tokens // GRID_W
    t = jnp.arange(rows * GRID_W)
    r = (t // GRID_W).astype(F32)
    col = (t % GRID_W).astype(F32)
    nf = dim // 4
    omega = 1.0 / (10000.0 ** (jnp.arange(nf, dtype=F32) / nf))
    ar = r[:, None] * omega
    ac = col[:, None] * omega
    return jnp.concatenate([jnp.sin(ar), jnp.cos(ar), jnp.sin(ac), jnp.cos(ac)], axis=-1)


def kernel(x_prompt, x_sample, state_hgrn, c, c_ctx, ln_in_g, ln_in_b, w_ada, b_ada, w_in, conv_w, conv_b, conv_ln_g, conv_ln_b, w_a_out, hgrn_lb, hgrn_norm_g, w_b_out, sgu_ln_g, sgu_ln_b, sgu_w, sgu_b, w_c_out, w_o, ln1_g, ln1_b, w_ffn_in, w_ffn_out, ln2_g, ln2_b):
    n_ctx_seq, ctx_len, d = x_prompt.shape
    n_lat_seq, lat_len, _ = x_sample.shape
    n_layers = w_in.shape[0]
    n_heads = state_hgrn.shape[3]
    geom = _Geom(n_ctx_seq, ctx_len, n_lat_seq, lat_len)
    alpha = float((2 * n_layers) ** 0.25)
    assert d == n_heads * LANES and w_in.shape[2] == N_BLK * d
    assert 1 + n_lat_seq <= COND_ROWS and ctx_len % CHUNK == 0 and lat_len % CHUNK == 0

    def row_tile(limit, within_seq):
        tm = CHUNK
        while (2 * tm <= limit and lat_len % (2 * tm) == 0
               and (ctx_len if within_seq else geom.t_ctx) % (2 * tm) == 0):
            tm *= 2
        return tm

    tm_mix = row_tile(256, False)
    tm_ffn = row_tile(512, False)
    tm_proj = row_tile(1024, False)
    n_sub = row_tile(256, True) // CHUNK

    cond = jnp.zeros((COND_ROWS, d), F32).at[0].set(c_ctx).at[1:1 + n_lat_seq].set(c)
    mod = _ada(cond, w_ada, b_ada)

    vec = lambda a: a.reshape(n_layers, 1, d)
    tok = _slab_tokens()
    pos = _to_slab_order(_sincos_2d(lat_len, d))
    x, m = _pre(geom, _to_slab_order(x_prompt.reshape(geom.t_ctx, d)), _to_slab_order(x_sample.reshape(geom.t_lat, d)),
                pos, ln_in_g.reshape(1, d), ln_in_b.reshape(1, d), mod, tm_ffn)

    lvl_f = jnp.asarray(_level_table())
    lvl_b = jnp.asarray(_level_table().T.copy())
    zeros_ctx = jnp.zeros((n_ctx_seq, n_heads, LANES, LANES), F32)
    p = dict(conv_ln_g=vec(conv_ln_g), conv_ln_b=vec(conv_ln_b),
             w_a_out=w_a_out, w_b_out=w_b_out, w_c_out=w_c_out, w_o=w_o, ln1_g=vec(ln1_g), ln1_b=vec(ln1_b))
    conv_bv, norm_g, sgu_g, sgu_bb = vec(conv_b), vec(hgrn_norm_g), vec(sgu_ln_g), vec(sgu_ln_b)
    ln2_gv, ln2_bv = vec(ln2_g), vec(ln2_b)
    w_ffn_in_bf16, w_ffn_out_bf16 = w_ffn_in.astype(BF16), w_ffn_out.astype(BF16)
    states = []
    for l in range(n_layers):
        h, vn = _proj_a(geom, m, w_in, sgu_g, sgu_bb, l, tm_ffn)
        (p_id, c0), (p_sig, c1), (p_silu, c2), (p_forget, c3) = (
            _proj_p(geom, m, w_in, hgrn_lb, h, conv_w, conv_bv, l, tm_proj, mode) for mode in _PROJ_MODES)
        s0_f = jnp.concatenate([zeros_ctx, state_hgrn[:, l, 0]], axis=0)
        s0_b = jnp.concatenate([zeros_ctx, state_hgrn[:, l, 1]], axis=0)
        o_f, sf_f = _hgrn(geom, p_id, p_silu, p_forget, s0_f, lvl_f, reverse=False, layer=l, n_sub=n_sub)
        ob, sf_b = _hgrn(geom, p_id, p_silu, p_forget, s0_b, lvl_b, reverse=True, layer=l, n_sub=n_sub,
                         o_fwd=o_f, norm_g=norm_g)
        states.append(jnp.stack([sf_f[:n_ctx_seq], sf_b[:n_ctx_seq]], axis=1))
        sw = sgu_w[l][:, tok][:, :, tok].astype(BF16)
        sb = sgu_b[l][:, tok][:, :, None]
        x1, m2 = _mix(geom, (c0, c1, c2, c3), p_id, p_sig, vn, ob, x, mod, p, sw, sb, l, tm_mix, alpha)
        out = _ffn(geom, m2, x1, mod, w_ffn_in_bf16, w_ffn_out_bf16, ln2_gv, ln2_bv, l, tm_ffn, alpha)
        x, m = (out[0], out[1]) if l + 1 < n_layers else (out[0], None)

    x = _from_slab_order(x)
    y_prompt = x[:geom.t_ctx].reshape(x_prompt.shape)
    y_sample = x[geom.t_ctx:].reshape(x_sample.shape)
    return y_prompt, y_sample, jnp.stack(states, axis=1)
```

```python
import functools

import jax
import jax.numpy as jnp
import numpy as np
from jax import lax
from jax.experimental import pallas as pl
from jax.experimental.pallas import tpu as pltpu

LN_EPS = 1e-5
GRID_W = 64
LANES = 128
SUBLANES = 8
CHUNK = 128
SLABS = CHUNK // SUBLANES
N_LEVELS = 7
COND_ROWS = 8
PROJ_SUB_ROWS = 256
PROJ_SUB_COLS = 256
VMEM_LIMIT = 56 * 1024 * 1024

F32 = jnp.float32
BF16 = jnp.bfloat16


def _cparams(*sem):
    return pltpu.CompilerParams(dimension_semantics=sem, vmem_limit_bytes=VMEM_LIMIT)


def _sigmoid(x):
    return 1.0 / (1.0 + jnp.exp(-x))


def _silu(x):
    return x * _sigmoid(x)


def _layernorm(x, g, b):
    mu = jnp.mean(x, axis=-1, keepdims=True)
    xc = x - mu
    var = jnp.mean(xc * xc, axis=-1, keepdims=True)
    return xc * lax.rsqrt(var + LN_EPS) * g + b


def _const_spec(shape):
    nd = len(shape)
    return pl.BlockSpec(shape, lambda *_: (0,) * nd, pipeline_mode=pl.Buffered(1))


def _layer_spec(shape, layer):
    nd = len(shape)
    return pl.BlockSpec((None,) + tuple(shape), lambda *_: (layer,) + (0,) * nd, pipeline_mode=pl.Buffered(1))


def _slab_tokens():
    p = np.arange(CHUNK)
    return SLABS * (p % SUBLANES) + p // SUBLANES


def _to_slab_order(x):
    t = x.shape[0]
    rest = x.shape[1:]
    return x.reshape((t // CHUNK, SUBLANES, SLABS) + rest).swapaxes(1, 2).reshape((t,) + rest)


def _from_slab_order(x):
    t = x.shape[0]
    rest = x.shape[1:]
    return x.reshape((t // CHUNK, SLABS, SUBLANES) + rest).swapaxes(1, 2).reshape((t,) + rest)


def _ada_kernel(cond_ref, w_ref, b_ref, o_ref):
    s = _silu(cond_ref[...]).astype(BF16)
    o_ref[0] = jnp.dot(s, w_ref[0].astype(BF16), preferred_element_type=F32) + b_ref[0]


def _ada(cond, w_ada, b_ada):
    n_layers, d, n6 = w_ada.shape
    tn = n6 // 4
    return pl.pallas_call(
        _ada_kernel,
        grid=(n_layers, n6 // tn),
        in_specs=[
            pl.BlockSpec((COND_ROWS, d), lambda l, j: (0, 0)),
            pl.BlockSpec((1, d, tn), lambda l, j: (l, 0, j)),
            pl.BlockSpec((1, 1, tn), lambda l, j: (l, 0, j)),
        ],
        out_specs=pl.BlockSpec((1, COND_ROWS, tn), lambda l, j: (l, 0, j)),
        out_shape=jax.ShapeDtypeStruct((n_layers, COND_ROWS, n6), F32),
        compiler_params=_cparams("arbitrary", "arbitrary"),
        name="ada",
    )(cond, w_ada, b_ada.reshape(n_layers, 1, n6))


class _Geom:
    def __init__(self, n_ctx_seq, ctx_len, n_lat_seq, lat_len):
        self.n_ctx_seq, self.ctx_len = n_ctx_seq, ctx_len
        self.n_lat_seq, self.lat_len = n_lat_seq, lat_len
        self.t_ctx = n_ctx_seq * ctx_len
        self.t_lat = n_lat_seq * lat_len
        self.t = self.t_ctx + self.t_lat
        self.n_seq = n_ctx_seq + n_lat_seq

    def cond_row(self, row0):
        return jnp.where(row0 < self.t_ctx, 0, 1 + (row0 - self.t_ctx) // self.lat_len)

    def seq_of(self, row0):
        return jnp.where(row0 < self.t_ctx, row0 // self.ctx_len,
                         self.n_ctx_seq + (row0 - self.t_ctx) // self.lat_len)

    def pos_in_seq(self, row0):
        return jnp.where(row0 < self.t_ctx, row0 % self.ctx_len, (row0 - self.t_ctx) % self.lat_len)

    def seq_len(self, row0):
        return jnp.where(row0 < self.t_ctx, self.ctx_len, self.lat_len)


def _mod_rows(mod_ref, cid, d, idx):
    return [mod_ref[pl.ds(cid, 1), pl.ds(i * d, d)] for i in idx]


def _pre_kernel(xc_ref, xl_ref, pos_ref, g_ref, b_ref, mod_ref, x_ref, m_ref, *, geom, tm, d):
    i = pl.program_id(0)
    row0 = i * tm
    cid = geom.cond_row(row0)
    sh1, sc1 = _mod_rows(mod_ref, cid, d, (0, 1))

    def emit(x):
        xn = _layernorm(x, g_ref[...], b_ref[...])
        x_ref[...] = xn
        m_ref[...] = (xn * (1.0 + sc1) + sh1).astype(BF16)

    @pl.when(row0 < geom.t_ctx)
    def _():
        emit(xc_ref[...])

    @pl.when(row0 >= geom.t_ctx)
    def _():
        emit(xl_ref[...] + pos_ref[...])


def _pre(geom, x_ctx, x_lat, pos, g, b, mod, tm):
    d = x_ctx.shape[-1]
    n_ctx_tiles = geom.t_ctx // tm
    pos_tiles = geom.lat_len // tm
    return pl.pallas_call(
        functools.partial(_pre_kernel, geom=geom, tm=tm, d=d),
        grid=(geom.t // tm,),
        in_specs=[
            pl.BlockSpec((tm, d), lambda i: (jnp.minimum(i, n_ctx_tiles - 1), 0)),
            pl.BlockSpec((tm, d), lambda i: (jnp.maximum(i - n_ctx_tiles, 0), 0)),
            pl.BlockSpec((tm, d), lambda i: (jnp.maximum(i - n_ctx_tiles, 0) % pos_tiles, 0)),
            _const_spec((1, d)), _const_spec((1, d)), _layer_spec(mod.shape[1:], 0),
        ],
        out_specs=[pl.BlockSpec((tm, d), lambda i: (i, 0))] * 2,
        out_shape=[jax.ShapeDtypeStruct((geom.t, d), F32), jax.ShapeDtypeStruct((geom.t, d), BF16)],
        compiler_params=_cparams("arbitrary"),
        name="pre",
    )(x_ctx, x_lat, pos, g, b, mod)


_W_VAL, _W_GATE, _W_Q, _W_ZF, _W_ZB, _W_I, _W_G, _W_U, _W_V, _W_GA, _W_GB, _W_GC = range(12)
N_BLK = 12
_PROJ_MODES = {
    "id": ((_W_U, _W_Q, _W_I), 0),
    "sig": ((_W_GA, _W_GB, _W_GC), 3),
    "silu": ((_W_G,), 6),
    "forget": ((_W_ZF, _W_ZB), 7),
}
_ID_U, _ID_Q, _ID_I = range(3)
_F_FWD, _F_BWD = range(2)


def _blk_lookup(blks, s):
    out = blks[-1]
    for k in range(len(blks) - 2, -1, -1):
        out = jnp.where(s == k, blks[k], out)
    return out


def _proj_a_kernel(m_ref, wv_ref, wg_ref, ws_ref, g_ref, b_ref, h_ref, vn_ref, wvb_ref, wgb_ref, wsb_ref):
    @pl.when(pl.program_id(0) == 0)
    def _():
        wvb_ref[...] = wv_ref[...].astype(BF16)
        wgb_ref[...] = wg_ref[...].astype(BF16)
        wsb_ref[...] = ws_ref[...].astype(BF16)

    for r0 in range(0, m_ref.shape[0], PROJ_SUB_ROWS):
        rs = pl.ds(r0, PROJ_SUB_ROWS)
        m = m_ref[rs, :]
        val = jnp.dot(m, wvb_ref[...], preferred_element_type=F32)
        gate = jnp.dot(m, wgb_ref[...], preferred_element_type=F32)
        hv = (val * _sigmoid(gate)).astype(BF16)
        for hh in range(h_ref.shape[0]):
            h_ref[hh, rs, :] = hv[:, hh * LANES:(hh + 1) * LANES]
        v = jnp.dot(m, wsb_ref[...], preferred_element_type=F32)
        vn_ref[rs, :] = _layernorm(v, g_ref[...], b_ref[...]).astype(BF16)


def _proj_a(geom, m, w_in, sgu_g, sgu_b, layer, tm):
    d = m.shape[-1]
    tok = pl.BlockSpec((tm, d), lambda i: (i, 0))

    def wblk(b):
        return pl.BlockSpec((None, d, d), lambda i: (layer, 0, b), pipeline_mode=pl.Buffered(1))

    return pl.pallas_call(
        _proj_a_kernel,
        grid=(geom.t // tm,),
        in_specs=[tok, wblk(_W_VAL), wblk(_W_GATE), wblk(_W_V), _layer_spec((1, d), layer), _layer_spec((1, d), layer)],
        out_specs=[pl.BlockSpec((d // LANES, tm, LANES), lambda i: (0, i, 0)), tok],
        out_shape=[jax.ShapeDtypeStruct((d // LANES, geom.t, LANES), BF16), jax.ShapeDtypeStruct((geom.t, d), BF16)],
        scratch_shapes=[pltpu.VMEM((d, d), BF16)] * 3,
        compiler_params=_cparams("arbitrary"),
        name=f"proj_a{layer}",
    )(m, w_in, w_in, w_in, sgu_g, sgu_b)


def _load_slabs(ref, c):
    out = []
    for p in range(SLABS // 2):
        x = ref[pl.ds(c * CHUNK + 2 * p * SUBLANES, 2 * SUBLANES), :].astype(F32)
        out += [x[:SUBLANES], x[SUBLANES:]]
    return out


def _conv_chunk(x, prev, nxt, w_ref, bias, has_prev, has_next):
    conv_k = w_ref.shape[0]
    half = conv_k // 2
    riota = lax.broadcasted_iota(jnp.int32, (SUBLANES, LANES), 0)
    w = [w_ref[k:k + 1, :] for k in range(conv_k)]
    up, down = [], []
    for j in range(SLABS):
        first_next = jnp.where(has_next, nxt[j][0:1, :], 0.0)
        last_prev = jnp.where(has_prev, prev[j][SUBLANES - 1:SUBLANES, :], 0.0)
        up.append(jnp.where(riota == SUBLANES - 1, first_next, pltpu.roll(x[j], SUBLANES - 1, 0)))
        down.append(jnp.where(riota == 0, last_prev, pltpu.roll(x[j], 1, 0)))
    outs = []
    for j in range(SLABS):
        acc = bias
        for k in range(conv_k):
            s = j + k - half
            term = x[s] if 0 <= s < SLABS else (up[s - SLABS] if s >= SLABS else down[s + SLABS])
            acc = acc + term * w[k]
        outs.append(acc)
    return outs


def _proj_p_kernel(m_ref, w_ref, lb_ref, h_ref, hp_ref, hn_ref, cw_ref, cb_ref, o_ref, conv_ref, wb_ref,
                   *, geom, layer, n_heads, mode, conv_rows, conv_split):
    k = pl.program_id(0)
    i = pl.program_id(1)

    @pl.when(i == 0)
    def _():
        wb_ref[...] = w_ref[...].astype(BF16)

    if mode == "forget":
        lb_all = lb_ref[...]
        e = jnp.exp(lb_all - jnp.max(lb_all, axis=0, keepdims=True))
        soft = e / jnp.sum(e, axis=0, keepdims=True)
        lb = jnp.sum(soft[:layer + 1], axis=0, keepdims=True) - soft[0:1]
    epilogue = {"id": lambda a: a, "sig": _sigmoid, "silu": _silu, "forget": None}[mode]

    tm, d = m_ref.shape
    n_conv = conv_rows // CHUNK
    row0 = (i * conv_split + k) * conv_rows if conv_split > 1 else i * conv_rows
    pieces = [(r0, n0) for r0 in range(0, tm, PROJ_SUB_ROWS) for n0 in range(0, d, PROJ_SUB_COLS)]
    conv_before = {c * len(pieces) // n_conv: c for c in range(n_conv)}
    for p, (r0, n0) in enumerate(pieces):
        if p in conv_before:
            c = conv_before[p]
            pos = geom.pos_in_seq(row0 + c * CHUNK)
            x = _load_slabs(h_ref, c)
            prev = _load_slabs(h_ref, c - 1) if c > 0 else _load_slabs(hp_ref, 0)
            nxt = _load_slabs(h_ref, c + 1) if c + 1 < n_conv else _load_slabs(hn_ref, 0)
            outs = _conv_chunk(x, prev, nxt, cw_ref, cb_ref[...], pos > 0,
                               pos + CHUNK < geom.seq_len(row0 + c * CHUNK))
            for j, o in enumerate(outs):
                conv_ref[pl.ds(c * CHUNK + j * SUBLANES, SUBLANES), :] = o

        rs = pl.ds(r0, PROJ_SUB_ROWS)
        y = jnp.dot(m_ref[rs, :], wb_ref[:, pl.ds(n0, PROJ_SUB_COLS)], preferred_element_type=F32)
        y = epilogue(y) if mode != "forget" else lb[:, n0:n0 + PROJ_SUB_COLS] + (
            1.0 - lb[:, n0:n0 + PROJ_SUB_COLS]) * _sigmoid(y)
        for hh in range(PROJ_SUB_COLS // LANES):
            o_ref[0, n0 // LANES + hh, rs, :] = y[:, hh * LANES:(hh + 1) * LANES].astype(o_ref.dtype)


def _proj_p(geom, m, w_in, hgrn_lb, h, conv_w, conv_b, layer, tm, mode):
    d = m.shape[-1]
    n_heads = d // LANES
    blks, first_col = _PROJ_MODES[mode]
    n_kind = len(blks)
    conv_k = conv_w.shape[1]
    assert conv_k // 2 < SLABS
    if mode == "forget":
        assert first_col == n_heads - 1
        conv_split, conv_cols, conv_rows = n_kind, 1, tm // n_kind
        lane_col = lambda k: first_col
        row_blk = lambda k, i: i * conv_split + k
        out_col = lambda k: 0
    else:
        conv_split, conv_cols, conv_rows = 1, n_kind, tm
        lane_col = lambda k: first_col + k
        row_blk = lambda k, i: i
        out_col = lambda k: k
    cpr = conv_rows // CHUNK
    n_chunks = geom.t // CHUNK
    prev_chunk = lambda k, i: jnp.maximum(row_blk(k, i) * cpr - 1, 0)
    next_chunk = lambda k, i: jnp.minimum((row_blk(k, i) + 1) * cpr, n_chunks - 1)
    return pl.pallas_call(
        functools.partial(_proj_p_kernel, geom=geom, layer=layer, n_heads=n_heads, mode=mode,
                          conv_rows=conv_rows, conv_split=conv_split),
        grid=(n_kind, geom.t // tm),
        in_specs=[
            pl.BlockSpec((tm, d), lambda k, i: (i, 0)),
            pl.BlockSpec((None, d, d), lambda k, i: (layer, 0, _blk_lookup(blks, k))),
            _const_spec(hgrn_lb.shape),
            pl.BlockSpec((None, conv_rows, LANES), lambda k, i: (lane_col(k), row_blk(k, i), 0)),
            pl.BlockSpec((None, CHUNK, LANES), lambda k, i: (lane_col(k), prev_chunk(k, i), 0)),
            pl.BlockSpec((None, CHUNK, LANES), lambda k, i: (lane_col(k), next_chunk(k, i), 0)),
            pl.BlockSpec((None, conv_k, LANES), lambda k, i: (layer, 0, lane_col(k))),
            pl.BlockSpec((None, 1, LANES), lambda k, i: (layer, 0, lane_col(k))),
        ],
        out_specs=[pl.BlockSpec((1, n_heads, tm, LANES), lambda k, i: (k, 0, i, 0)),
                   pl.BlockSpec((None, conv_rows, LANES), lambda k, i: (out_col(k), row_blk(k, i), 0))],
        out_shape=[jax.ShapeDtypeStruct((n_kind, n_heads, geom.t, LANES), F32 if mode == "forget" else BF16),
                   jax.ShapeDtypeStruct((conv_cols, geom.t, LANES), F32)],
        scratch_shapes=[pltpu.VMEM((d, d), BF16)],
        compiler_params=_cparams("arbitrary", "arbitrary"),
        name=f"proj_{mode}{layer}",
    )(m, w_in, hgrn_lb, h, h, h, conv_w, conv_b)


def _level_table():
    tok = _slab_tokens()
    t, s = tok[:, None], tok[None, :]
    x = t ^ s
    lvl = np.zeros((CHUNK, CHUNK), np.int32)
    for k in range(N_LEVELS):
        lvl = np.where((x >> k) > 0, k + 1, lvl)
    lvl = np.where(s > t, -1, lvl)
    return lvl.astype(np.int32)


def _row_bcast(x, r):
    return jnp.broadcast_to(x[r:r + 1, :], x.shape)


def _sublane_gather(src, rows_from, riota):
    out = jnp.ones_like(src)
    for r, rf in enumerate(rows_from):
        if rf is not None:
            out = jnp.where(riota == r, _row_bcast(src, rf), out)
    return out


def _hgrn_head(q, f, v_cat, st, lvl, *, reverse):
    kk_vec = [1.0 - fj for fj in f]
    riota = lax.broadcasted_iota(jnp.int32, (SUBLANES, LANES), 0)
    tgt = list(f)
    src = [jnp.ones_like(fj) for fj in f]

    def cat_bf16(slabs):
        return jnp.concatenate(slabs, axis=0).astype(BF16)

    diag = jnp.sum(jnp.concatenate([q[j] * kk_vec[j] for j in range(SLABS)], axis=0), axis=-1, keepdims=True)
    a = jnp.where(lvl == 0, diag, 0.0)
    for k in range(N_LEVELS):
        p = lax.dot_general(cat_bf16([q[j] * tgt[j] for j in range(SLABS)]),
                            cat_bf16([kk_vec[j] * src[j] for j in range(SLABS)]),
                            (((1,), (1,)), ((), ())), preferred_element_type=F32)
        a = jnp.where(lvl == k + 1, p, a)
        if k < 4:
            bit, low = 1 << k, (1 << k) - 1
            new_tgt, new_src = list(tgt), list(src)
            for j in range(SLABS):
                in_far_half = bool(j & bit) != reverse
                if in_far_half:
                    sib_edge = ((j & ~bit) | low) if not reverse else ((j | bit) & ~low)
                    new_tgt[j] = tgt[j] * tgt[sib_edge]
                else:
                    sib_edge = ((j | bit) | low) if not reverse else ((j & ~bit) & ~low)
                    new_src[j] = src[j] * tgt[sib_edge]
            tgt, src = new_tgt, new_src
        else:
            bit, low = 1 << (k - 4), (1 << (k - 4)) - 1
            edge = tgt[SLABS - 1] if not reverse else tgt[0]
            rows_t, rows_s = [], []
            for r in range(SUBLANES):
                in_far_half = bool(r & bit) != reverse
                if in_far_half:
                    rows_t.append(((r & ~bit) | low) if not reverse else ((r | bit) & ~low))
                    rows_s.append(None)
                else:
                    rows_t.append(None)
                    rows_s.append(((r | bit) | low) if not reverse else ((r & ~bit) & ~low))
            fac_t = _sublane_gather(edge, rows_t, riota)
            fac_s = _sublane_gather(edge, rows_s, riota)
            tgt = [x * fac_t for x in tgt]
            src = [x * fac_s for x in src]

    lhs = jnp.concatenate([a.astype(BF16), cat_bf16([q[j] * tgt[j] for j in range(SLABS)])], axis=1)
    rhs = jnp.concatenate([v_cat, st.astype(BF16)], axis=0)
    o = jnp.dot(lhs, rhs, preferred_element_type=F32)
    edge = tgt[SLABS - 1] if not reverse else tgt[0]
    row = SUBLANES - 1 if not reverse else 0
    total = jnp.transpose(jnp.broadcast_to(edge[row:row + 1, :], (LANES, LANES)))
    upd = lax.dot_general(cat_bf16([kk_vec[j] * src[j] for j in range(SLABS)]), v_cat,
                          (((0,), (0,)), ((), ())), preferred_element_type=F32)
    return o, st * total + upd


def _hgrn_kernel(*refs, geom, reverse, n_heads, n_sub):
    if reverse:
        q_ref, v_ref, sg_ref, f_ref, of_ref, ng_ref, lvl_ref, s0_ref, o_ref, sf_ref, st_ref = refs
    else:
        q_ref, v_ref, f_ref, lvl_ref, s0_ref, o_ref, sf_ref, st_ref = refs
    rows = n_sub * CHUNK
    c = pl.program_id(0)
    n_steps = geom.t // rows
    ce = (n_steps - 1 - c) if reverse else c
    row0 = ce * rows
    pos = geom.pos_in_seq(row0)
    at_start = pos == 0
    at_end = pos == geom.seq_len(row0) - rows
    is_first, is_last = (at_end, at_start) if reverse else (at_start, at_end)

    is_ctx = row0 < geom.t_ctx

    @pl.when(is_first)
    def _():
        st_ref[...] = jnp.where(is_ctx, 0.0, s0_ref[...])

    lvl = lvl_ref[...]
    slab = lambda x: [x[j * SUBLANES:(j + 1) * SUBLANES, :] for j in range(SLABS)]
    for cc in (reversed(range(n_sub)) if reverse else range(n_sub)):
        rs = pl.ds(cc * CHUNK, CHUNK)
        for h in range(n_heads):
            q32 = q_ref[0, h, rs, :].astype(F32)
            o, st_new = _hgrn_head(slab(q32), slab(f_ref[0, h, rs, :]), v_ref[0, h, rs, :], st_ref[h], lvl,
                                   reverse=reverse)
            st_ref[h] = st_new
            if reverse:
                o = o + of_ref[h, rs, :]
                o = o * lax.rsqrt(jnp.mean(o * o, axis=-1, keepdims=True) + LN_EPS)
                cols = pl.ds(h * LANES, LANES)
                o_ref[rs, cols] = (o * ng_ref[:, cols] * sg_ref[0, h, rs, :].astype(F32)).astype(BF16)
            else:
                o_ref[h, rs, :] = o

    @pl.when(jnp.logical_and(is_last, is_ctx))
    def _():
        sf_ref[0] = st_ref[...]


def _hgrn(geom, p_id, p_silu, p_forget, state_hgrn, lvl, *, reverse, layer, n_sub, o_fwd=None, norm_g=None):
    _, n_heads, t, _ = p_id.shape
    d = n_heads * LANES
    rows = n_sub * CHUNK
    n_steps = t // rows

    def step(c):
        return (n_steps - 1 - c) if reverse else c

    def blk(kind):
        return pl.BlockSpec((1, n_heads, rows, LANES), lambda c: (kind, 0, step(c), 0))

    head_rows = pl.BlockSpec((n_heads, rows, LANES), lambda c: (0, step(c), 0))
    seq = lambda c: geom.seq_of(step(c) * rows)
    direction = 1 if reverse else 0
    s0_spec = pl.BlockSpec((None, None, None, n_heads, LANES, LANES),
                           lambda c: (jnp.maximum(seq(c) - geom.n_ctx_seq, 0), layer, direction, 0, 0, 0))
    state_spec = pl.BlockSpec((1, n_heads, LANES, LANES), lambda c: (jnp.minimum(seq(c), geom.n_ctx_seq - 1), 0, 0, 0))
    lvl_spec = _const_spec((CHUNK, CHUNK))
    state_shape = jax.ShapeDtypeStruct((geom.n_ctx_seq, n_heads, LANES, LANES), F32)
    if reverse:
        in_specs = [blk(_ID_Q), blk(_ID_I), blk(0), blk(_F_BWD), head_rows, _layer_spec((1, d), layer),
                    lvl_spec, s0_spec]
        args = (p_id, p_id, p_silu, p_forget, o_fwd, norm_g, lvl, state_hgrn)
        out_specs = [pl.BlockSpec((rows, d), lambda c: (step(c), 0)), state_spec]
        out_shape = [jax.ShapeDtypeStruct((t, d), BF16), state_shape]
    else:
        in_specs = [blk(_ID_Q), blk(_ID_I), blk(_F_FWD), lvl_spec, s0_spec]
        args = (p_id, p_id, p_forget, lvl, state_hgrn)
        out_specs = [head_rows, state_spec]
        out_shape = [jax.ShapeDtypeStruct((n_heads, t, LANES), F32), state_shape]
    return pl.pallas_call(
        functools.partial(_hgrn_kernel, geom=geom, reverse=reverse, n_heads=n_heads, n_sub=n_sub),
        grid=(n_steps,),
        in_specs=in_specs,
        out_specs=out_specs,
        out_shape=out_shape,
        scratch_shapes=[pltpu.VMEM((n_heads, LANES, LANES), F32)],
        compiler_params=_cparams("arbitrary"),
        name=f"hgrn{layer}{'b' if reverse else 'f'}",
    )(*args)


def _mix_kernel(c0_ref, c1_ref, c2_ref, c3_ref, u_ref, vn_ref, ga_ref, gb_ref, gc_ref, ob_ref, x_ref, mod_ref,
                clg_ref, clb_ref, wa_ref, wb_ref, sw_ref, sb_ref, wc_ref, wo_ref, l1g_ref, l1b_ref,
                x1_ref, m2_ref, wab_ref, wbb_ref, wcb_ref, wob_ref, *, geom, tm, d, alpha):
    i = pl.program_id(0)
    cid = geom.cond_row(i * tm)
    n_heads = d // LANES

    @pl.when(i == 0)
    def _():
        wab_ref[...] = wa_ref[...].astype(BF16)
        wbb_ref[...] = wb_ref[...].astype(BF16)
        wcb_ref[...] = wc_ref[...].astype(BF16)
        wob_ref[...] = wo_ref[...].astype(BF16)

    def tokens(ref):
        return jnp.concatenate([ref[0, h] for h in range(n_heads)], axis=-1).astype(F32)

    mix = tokens(gb_ref) * jnp.dot(ob_ref[...], wbb_ref[...], preferred_element_type=F32)

    vn = vn_ref[...]
    rows = []
    for ch in range(tm // CHUNK):
        cols = []
        for g in range(n_heads):
            vg = vn[ch * CHUNK:(ch + 1) * CHUNK, g * LANES:(g + 1) * LANES]
            cols.append(jnp.dot(sw_ref[g], vg, preferred_element_type=F32) + sb_ref[g])
        rows.append(jnp.concatenate(cols, axis=-1))
    mixed = jnp.concatenate(rows, axis=0)
    yc = jnp.dot((tokens(u_ref) * mixed).astype(BF16), wcb_ref[...], preferred_element_type=F32)
    mix += tokens(gc_ref) * yc

    conv = jnp.concatenate([r[c] for r in (c0_ref, c1_ref, c2_ref, c3_ref) for c in range(r.shape[0])], axis=-1)
    ha = _silu(_layernorm(conv, clg_ref[...], clb_ref[...]))
    mix += tokens(ga_ref) * jnp.dot(ha.astype(BF16), wab_ref[...], preferred_element_type=F32)

    y = jnp.dot(mix.astype(BF16), wob_ref[...], preferred_element_type=F32)
    g1, sh2, sc2 = _mod_rows(mod_ref, cid, d, (2, 3, 4))
    x1 = _layernorm(alpha * x_ref[...] + g1 * y, l1g_ref[...], l1b_ref[...])
    x1_ref[...] = x1
    m2_ref[...] = (x1 * (1.0 + sc2) + sh2).astype(BF16)


def _mix(geom, convs, p_id, p_sig, vn, ob, x, mod, p, sgu_w, sgu_b, layer, tm, alpha):
    t, d = x.shape
    n_heads = d // LANES
    n_g = sgu_w.shape[0]
    assert tm % CHUNK == 0 and n_g * LANES == d and sum(c.shape[0] for c in convs) == n_heads

    def kind(k):
        return pl.BlockSpec((1, n_heads, tm, LANES), lambda i: (k, 0, i, 0))

    tok = pl.BlockSpec((tm, d), lambda i: (i, 0))
    vec = _layer_spec((1, d), layer)
    mat = _layer_spec((d, d), layer)
    in_specs = [pl.BlockSpec((c.shape[0], tm, LANES), lambda i: (0, i, 0)) for c in convs]
    in_specs += [kind(_ID_U), tok, kind(0), kind(1), kind(2), tok, tok,
                _layer_spec(mod.shape[1:], layer),
                vec, vec, mat, mat,
                _const_spec((n_g, CHUNK, CHUNK)), _const_spec((n_g, CHUNK, 1)), mat, mat, vec, vec]
    return pl.pallas_call(
        functools.partial(_mix_kernel, geom=geom, tm=tm, d=d, alpha=alpha),
        grid=(t // tm,),
        in_specs=in_specs,
        out_specs=[tok, tok],
        out_shape=[jax.ShapeDtypeStruct((t, d), F32), jax.ShapeDtypeStruct((t, d), BF16)],
        scratch_shapes=[pltpu.VMEM((d, d), BF16)] * 4,
        compiler_params=_cparams("arbitrary"),
        name=f"mix{layer}",
    )(*convs, p_id, vn, p_sig, p_sig, p_sig, ob, x, mod,
      p["conv_ln_g"], p["conv_ln_b"], p["w_a_out"], p["w_b_out"],
      sgu_w, sgu_b, p["w_c_out"], p["w_o"], p["ln1_g"], p["ln1_b"])


def _ffn_kernel(m2_ref, x1_ref, mod_ref, modn_ref, wi_ref, wo_ref, g_ref, b_ref, x2_ref, *mn_ref,
                geom, tm, d, f_hidden, f_chunk, alpha):
    i = pl.program_id(0)
    cid = geom.cond_row(i * tm)
    (g2,) = _mod_rows(mod_ref, cid, d, (5,))
    for r0 in range(0, tm, PROJ_SUB_ROWS):
        rs = pl.ds(r0, PROJ_SUB_ROWS)
        m2 = m2_ref[rs, :]
        acc = jnp.zeros((PROJ_SUB_ROWS, d), F32)
        for c0 in range(0, f_hidden, f_chunk):
            gte = jnp.dot(m2, wi_ref[:, pl.ds(c0, f_chunk)], preferred_element_type=F32)
            up = jnp.dot(m2, wi_ref[:, pl.ds(f_hidden + c0, f_chunk)], preferred_element_type=F32)
            acc += jnp.dot((_silu(gte) * up).astype(BF16), wo_ref[pl.ds(c0, f_chunk), :],
                           preferred_element_type=F32)
        x2 = _layernorm(alpha * x1_ref[rs, :] + g2 * acc, g_ref[...], b_ref[...])
        x2_ref[rs, :] = x2
        if mn_ref:
            sh1, sc1 = _mod_rows(modn_ref, cid, d, (0, 1))
            mn_ref[0][rs, :] = (x2 * (1.0 + sc1) + sh1).astype(BF16)


def _ffn(geom, m2, x1, mod, w_ffn_in_bf16, w_ffn_out_bf16, ln2_g, ln2_b, layer, tm, alpha):
    t, d = x1.shape
    n_layers, f_hidden, _ = w_ffn_out_bf16.shape
    f_chunk = f_hidden // 2
    assert f_chunk % LANES == 0
    emit_next = layer + 1 < n_layers
    tok = pl.BlockSpec((tm, d), lambda i: (i, 0))
    out_shape = [jax.ShapeDtypeStruct((t, d), F32)] + ([jax.ShapeDtypeStruct((t, d), BF16)] if emit_next else [])
    return pl.pallas_call(
        functools.partial(_ffn_kernel, geom=geom, tm=tm, d=d, f_hidden=f_hidden, f_chunk=f_chunk, alpha=alpha),
        grid=(t // tm,),
        in_specs=[tok, tok, _layer_spec(mod.shape[1:], layer),
                  _layer_spec(mod.shape[1:], layer + 1 if emit_next else layer),
                  _layer_spec((d, 2 * f_hidden), layer), _layer_spec((f_hidden, d), layer),
                  _layer_spec((1, d), layer), _layer_spec((1, d), layer)],
        out_specs=[tok] * len(out_shape),
        out_shape=out_shape,
        compiler_params=_cparams("arbitrary"),
        name=f"ffn{layer}",
    )(m2, x1, mod, mod, w_ffn_in_bf16, w_ffn_out_bf16, ln2_g, ln2_b)


def _sincos_2d(n_tokens, dim):
    rows = n_tokens // GRID_W
    t = jnp.arange(rows * GRID_W)
    r = (t // GRID_W).astype(F32)
    col = (t % GRID_W).astype(F32)
    nf = dim // 4
    omega = 1.0 / (10000.0 ** (jnp.arange(nf, dtype=F32) / nf))
    ar = r[:, None] * omega
    ac = col[:, None] * omega
    return jnp.concatenate([jnp.sin(ar), jnp.cos(ar), jnp.sin(ac), jnp.cos(ac)], axis=-1)


def kernel(x_prompt, x_sample, state_hgrn, c, c_ctx, ln_in_g, ln_in_b, w_ada, b_ada, w_in, conv_w, conv_b, conv_ln_g, conv_ln_b, w_a_out, hgrn_lb, hgrn_norm_g, w_b_out, sgu_ln_g, sgu_ln_b, sgu_w, sgu_b, w_c_out, w_o, ln1_g, ln1_b, w_ffn_in, w_ffn_out, ln2_g, ln2_b):
    n_ctx_seq, ctx_len, d = x_prompt.shape
    n_lat_seq, lat_len, _ = x_sample.shape
    n_layers = w_in.shape[0]
    n_heads = state_hgrn.shape[3]
    geom = _Geom(n_ctx_seq, ctx_len, n_lat_seq, lat_len)
    alpha = float((2 * n_layers) ** 0.25)
    assert d == n_heads * LANES and w_in.shape[2] == N_BLK * d
    assert 1 + n_lat_seq <= COND_ROWS and ctx_len % CHUNK == 0 and lat_len % CHUNK == 0

    def row_tile(limit, within_seq):
        tm = CHUNK
        while (2 * tm <= limit and lat_len % (2 * tm) == 0
               and (ctx_len if within_seq else geom.t_ctx) % (2 * tm) == 0):
            tm *= 2
        return tm

    tm_mix = row_tile(256, False)
    tm_ffn = row_tile(512, False)
    tm_proj = row_tile(1024, False)
    n_sub = row_tile(256, True) // CHUNK

    cond = jnp.zeros((COND_ROWS, d), F32).at[0].set(c_ctx).at[1:1 + n_lat_seq].set(c)
    mod = _ada(cond, w_ada, b_ada)

    vec = lambda a: a.reshape(n_layers, 1, d)
    tok = _slab_tokens()
    pos = _to_slab_order(_sincos_2d(lat_len, d))
    x, m = _pre(geom, _to_slab_order(x_prompt.reshape(geom.t_ctx, d)), _to_slab_order(x_sample.reshape(geom.t_lat, d)),
                pos, ln_in_g.reshape(1, d), ln_in_b.reshape(1, d), mod, tm_ffn)

    lvl_f = jnp.asarray(_level_table())
    lvl_b = jnp.asarray(_level_table().T.copy())
    p = dict(conv_ln_g=vec(conv_ln_g), conv_ln_b=vec(conv_ln_b),
             w_a_out=w_a_out, w_b_out=w_b_out, w_c_out=w_c_out, w_o=w_o, ln1_g=vec(ln1_g), ln1_b=vec(ln1_b))
    conv_bv, norm_g, sgu_g, sgu_bb = vec(conv_b), vec(hgrn_norm_g), vec(sgu_ln_g), vec(sgu_ln_b)
    ln2_gv, ln2_bv = vec(ln2_g), vec(ln2_b)
    w_ffn_in_bf16, w_ffn_out_bf16 = w_ffn_in.astype(BF16), w_ffn_out.astype(BF16)
    states = []
    for l in range(n_layers):
        h, vn = _proj_a(geom, m, w_in, sgu_g, sgu_bb, l, tm_ffn)
        (p_id, c0), (p_sig, c1), (p_silu, c2), (p_forget, c3) = (
            _proj_p(geom, m, w_in, hgrn_lb, h, conv_w, conv_bv, l, tm_proj, mode) for mode in _PROJ_MODES)
        o_f, sf_f = _hgrn(geom, p_id, p_silu, p_forget, state_hgrn, lvl_f, reverse=False, layer=l, n_sub=n_sub)
        ob, sf_b = _hgrn(geom, p_id, p_silu, p_forget, state_hgrn, lvl_b, reverse=True, layer=l, n_sub=n_sub,
                         o_fwd=o_f, norm_g=norm_g)
        states.append(jnp.stack([sf_f, sf_b], axis=1))
        sw = sgu_w[l][:, tok][:, :, tok].astype(BF16)
        sb = sgu_b[l][:, tok][:, :, None]
        x1, m2 = _mix(geom, (c0, c1, c2, c3), p_id, p_sig, vn, ob, x, mod, p, sw, sb, l, tm_mix, alpha)
        out = _ffn(geom, m2, x1, mod, w_ffn_in_bf16, w_ffn_out_bf16, ln2_gv, ln2_bv, l, tm_ffn, alpha)
        x, m = (out[0], out[1]) if l + 1 < n_layers else (out[0], None)

    x = _from_slab_order(x)
    y_prompt = x[:geom.t_ctx].reshape(x_prompt.shape)
    y_sample = x[geom.t_ctx:].reshape(x_sample.shape)
    return y_prompt, y_sample, jnp.stack(states, axis=1)
```

```python
import functools

import jax
import jax.numpy as jnp
import numpy as np
from jax import lax
from jax.experimental import pallas as pl
from jax.experimental.pallas import tpu as pltpu

LN_EPS = 1e-5
GRID_W = 64
LANES = 128
SUBLANES = 8
CHUNK = 128
SLABS = CHUNK // SUBLANES
N_LEVELS = 7
COND_ROWS = 8
HGRN_GROUP = 8
PROJ_SUB_ROWS = 256
PROJ_SUB_COLS = 256
VMEM_LIMIT = 56 * 1024 * 1024

F32 = jnp.float32
BF16 = jnp.bfloat16


def _cparams(*sem):
    return pltpu.CompilerParams(dimension_semantics=sem, vmem_limit_bytes=VMEM_LIMIT)


def _sigmoid(x):
    return 1.0 / (1.0 + jnp.exp(-x))


def _silu(x):
    return x * _sigmoid(x)


def _layernorm(x, g, b):
    mu = jnp.mean(x, axis=-1, keepdims=True)
    xc = x - mu
    var = jnp.mean(xc * xc, axis=-1, keepdims=True)
    return xc * lax.rsqrt(var + LN_EPS) * g + b


def _const_spec(shape):
    nd = len(shape)
    return pl.BlockSpec(shape, lambda *_: (0,) * nd, pipeline_mode=pl.Buffered(1))


def _layer_spec(shape, layer):
    nd = len(shape)
    return pl.BlockSpec((None,) + tuple(shape), lambda *_: (layer,) + (0,) * nd, pipeline_mode=pl.Buffered(1))


def _slab_tokens():
    p = np.arange(CHUNK)
    return SLABS * (p % SUBLANES) + p // SUBLANES


def _to_slab_order(x):
    t = x.shape[0]
    rest = x.shape[1:]
    return x.reshape((t // CHUNK, SUBLANES, SLABS) + rest).swapaxes(1, 2).reshape((t,) + rest)


def _from_slab_order(x):
    t = x.shape[0]
    rest = x.shape[1:]
    return x.reshape((t // CHUNK, SLABS, SUBLANES) + rest).swapaxes(1, 2).reshape((t,) + rest)


def _ada_kernel(cond_ref, w_ref, b_ref, o_ref):
    s = _silu(cond_ref[...]).astype(BF16)
    o_ref[0] = jnp.dot(s, w_ref[0].astype(BF16), preferred_element_type=F32) + b_ref[0]


def _ada(cond, w_ada, b_ada):
    n_layers, d, n6 = w_ada.shape
    tn = n6 // 4
    return pl.pallas_call(
        _ada_kernel,
        grid=(n_layers, n6 // tn),
        in_specs=[
            pl.BlockSpec((COND_ROWS, d), lambda l, j: (0, 0)),
            pl.BlockSpec((1, d, tn), lambda l, j: (l, 0, j)),
            pl.BlockSpec((1, 1, tn), lambda l, j: (l, 0, j)),
        ],
        out_specs=pl.BlockSpec((1, COND_ROWS, tn), lambda l, j: (l, 0, j)),
        out_shape=jax.ShapeDtypeStruct((n_layers, COND_ROWS, n6), F32),
        compiler_params=_cparams("arbitrary", "arbitrary"),
        name="ada",
    )(cond, w_ada, b_ada.reshape(n_layers, 1, n6))


class _Geom:
    def __init__(self, n_ctx_seq, ctx_len, n_lat_seq, lat_len):
        self.n_ctx_seq, self.ctx_len = n_ctx_seq, ctx_len
        self.n_lat_seq, self.lat_len = n_lat_seq, lat_len
        self.t_ctx = n_ctx_seq * ctx_len
        self.t_lat = n_lat_seq * lat_len
        self.t = self.t_ctx + self.t_lat
        self.n_seq = n_ctx_seq + n_lat_seq

    def cond_row(self, row0):
        return jnp.where(row0 < self.t_ctx, 0, 1 + (row0 - self.t_ctx) // self.lat_len)

    def seq_of(self, row0):
        return jnp.where(row0 < self.t_ctx, row0 // self.ctx_len,
                         self.n_ctx_seq + (row0 - self.t_ctx) // self.lat_len)

    def pos_in_seq(self, row0):
        return jnp.where(row0 < self.t_ctx, row0 % self.ctx_len, (row0 - self.t_ctx) % self.lat_len)

    def seq_len(self, row0):
        return jnp.where(row0 < self.t_ctx, self.ctx_len, self.lat_len)


def _mod_rows(mod_ref, cid, d, idx):
    return [mod_ref[pl.ds(cid, 1), pl.ds(i * d, d)] for i in idx]


def _pre_kernel(xc_ref, xl_ref, pos_ref, g_ref, b_ref, mod_ref, x_ref, m_ref, *, geom, tm, d):
    i = pl.program_id(0)
    row0 = i * tm
    cid = geom.cond_row(row0)
    sh1, sc1 = _mod_rows(mod_ref, cid, d, (0, 1))

    def emit(x):
        xn = _layernorm(x, g_ref[...], b_ref[...])
        x_ref[...] = xn
        m_ref[...] = (xn * (1.0 + sc1) + sh1).astype(BF16)

    @pl.when(row0 < geom.t_ctx)
    def _():
        emit(xc_ref[...])

    @pl.when(row0 >= geom.t_ctx)
    def _():
        emit(xl_ref[...] + pos_ref[...])


def _pre(geom, x_ctx, x_lat, pos, g, b, mod, tm):
    d = x_ctx.shape[-1]
    n_ctx_tiles = geom.t_ctx // tm
    pos_tiles = geom.lat_len // tm
    return pl.pallas_call(
        functools.partial(_pre_kernel, geom=geom, tm=tm, d=d),
        grid=(geom.t // tm,),
        in_specs=[
            pl.BlockSpec((tm, d), lambda i: (jnp.minimum(i, n_ctx_tiles - 1), 0)),
            pl.BlockSpec((tm, d), lambda i: (jnp.maximum(i - n_ctx_tiles, 0), 0)),
            pl.BlockSpec((tm, d), lambda i: (jnp.maximum(i - n_ctx_tiles, 0) % pos_tiles, 0)),
            _const_spec((1, d)), _const_spec((1, d)), _layer_spec(mod.shape[1:], 0),
        ],
        out_specs=[pl.BlockSpec((tm, d), lambda i: (i, 0))] * 2,
        out_shape=[jax.ShapeDtypeStruct((geom.t, d), F32), jax.ShapeDtypeStruct((geom.t, d), BF16)],
        compiler_params=_cparams("arbitrary"),
        name="pre",
    )(x_ctx, x_lat, pos, g, b, mod)


_W_VAL, _W_GATE, _W_Q, _W_ZF, _W_ZB, _W_I, _W_G, _W_U, _W_V, _W_GA, _W_GB, _W_GC = range(12)
N_BLK = 12
_PROJ_MODES = {
    "id": ((_W_U, _W_Q, _W_I), 0),
    "sig": ((_W_GA, _W_GB, _W_GC), 3),
    "silu": ((_W_G,), 6),
    "forget": ((_W_ZF, _W_ZB), 7),
}
_ID_U, _ID_Q, _ID_I = range(3)
_F_FWD, _F_BWD = range(2)


def _blk_lookup(blks, s):
    out = blks[-1]
    for k in range(len(blks) - 2, -1, -1):
        out = jnp.where(s == k, blks[k], out)
    return out


def _proj_a_kernel(m_ref, wv_ref, wg_ref, ws_ref, g_ref, b_ref, h_ref, vn_ref, wvb_ref, wgb_ref, wsb_ref):
    @pl.when(pl.program_id(0) == 0)
    def _():
        wvb_ref[...] = wv_ref[...].astype(BF16)
        wgb_ref[...] = wg_ref[...].astype(BF16)
        wsb_ref[...] = ws_ref[...].astype(BF16)

    for r0 in range(0, m_ref.shape[0], PROJ_SUB_ROWS):
        rs = pl.ds(r0, PROJ_SUB_ROWS)
        m = m_ref[rs, :]
        val = jnp.dot(m, wvb_ref[...], preferred_element_type=F32)
        gate = jnp.dot(m, wgb_ref[...], preferred_element_type=F32)
        hv = (val * _sigmoid(gate)).astype(BF16)
        for hh in range(h_ref.shape[0]):
            h_ref[hh, rs, :] = hv[:, hh * LANES:(hh + 1) * LANES]
        v = jnp.dot(m, wsb_ref[...], preferred_element_type=F32)
        vn_ref[rs, :] = _layernorm(v, g_ref[...], b_ref[...]).astype(BF16)


def _proj_a(geom, m, w_in, sgu_g, sgu_b, layer, tm):
    d = m.shape[-1]
    tok = pl.BlockSpec((tm, d), lambda i: (i, 0))

    def wblk(b):
        return pl.BlockSpec((None, d, d), lambda i: (layer, 0, b), pipeline_mode=pl.Buffered(1))

    return pl.pallas_call(
        _proj_a_kernel,
        grid=(geom.t // tm,),
        in_specs=[tok, wblk(_W_VAL), wblk(_W_GATE), wblk(_W_V), _layer_spec((1, d), layer), _layer_spec((1, d), layer)],
        out_specs=[pl.BlockSpec((d // LANES, tm, LANES), lambda i: (0, i, 0)), tok],
        out_shape=[jax.ShapeDtypeStruct((d // LANES, geom.t, LANES), BF16), jax.ShapeDtypeStruct((geom.t, d), BF16)],
        scratch_shapes=[pltpu.VMEM((d, d), BF16)] * 3,
        compiler_params=_cparams("arbitrary"),
        name=f"proj_a{layer}",
    )(m, w_in, w_in, w_in, sgu_g, sgu_b)


def _load_slabs(ref, c):
    out = []
    for p in range(SLABS // 2):
        x = ref[pl.ds(c * CHUNK + 2 * p * SUBLANES, 2 * SUBLANES), :].astype(F32)
        out += [x[:SUBLANES], x[SUBLANES:]]
    return out


def _conv_chunk(x, prev, nxt, w_ref, bias, has_prev, has_next):
    conv_k = w_ref.shape[0]
    half = conv_k // 2
    riota = lax.broadcasted_iota(jnp.int32, (SUBLANES, LANES), 0)
    up, down = [], []
    for j in range(SLABS):
        first_next = jnp.where(has_next, nxt[j][0:1, :], 0.0)
        last_prev = jnp.where(has_prev, prev[j][SUBLANES - 1:SUBLANES, :], 0.0)
        up.append(jnp.where(riota == SUBLANES - 1, first_next, pltpu.roll(x[j], SUBLANES - 1, 0)))
        down.append(jnp.where(riota == 0, last_prev, pltpu.roll(x[j], 1, 0)))
    group = SLABS // 2
    outs = []
    for j0 in range(0, SLABS, group):
        accs = [bias] * group
        for k in range(conv_k):
            wk = jnp.broadcast_to(w_ref[k:k + 1, :], (SUBLANES, LANES))
            for jj in range(group):
                s = j0 + jj + k - half
                term = x[s] if 0 <= s < SLABS else (up[s - SLABS] if s >= SLABS else down[s + SLABS])
                accs[jj] = accs[jj] + term * wk
        outs += accs
    return outs


def _proj_p_kernel(m_ref, w_ref, lb_ref, h_ref, hp_ref, hn_ref, cw_ref, cb_ref, o_ref, conv_ref, wb_ref,
                   *, geom, layer, n_heads, mode, conv_rows, conv_split):
    k = pl.program_id(0)
    i = pl.program_id(1)

    @pl.when(i == 0)
    def _():
        wb_ref[...] = w_ref[...].astype(BF16)

    if mode == "forget":
        lb_all = lb_ref[...]
        e = jnp.exp(lb_all - jnp.max(lb_all, axis=0, keepdims=True))
        soft = e / jnp.sum(e, axis=0, keepdims=True)
        lb = jnp.sum(soft[:layer + 1], axis=0, keepdims=True) - soft[0:1]
    epilogue = {"id": lambda a: a, "sig": _sigmoid, "silu": _silu, "forget": None}[mode]

    tm, d = m_ref.shape
    n_conv = conv_rows // CHUNK
    row0 = (i * conv_split + k) * conv_rows if conv_split > 1 else i * conv_rows
    pieces = [(r0, n0) for r0 in range(0, tm, PROJ_SUB_ROWS) for n0 in range(0, d, PROJ_SUB_COLS)]
    conv_before = {c * len(pieces) // n_conv: c for c in range(n_conv)}
    for p, (r0, n0) in enumerate(pieces):
        if p in conv_before:
            c = conv_before[p]
            pos = geom.pos_in_seq(row0 + c * CHUNK)
            x = _load_slabs(h_ref, c)
            prev = _load_slabs(h_ref, c - 1) if c > 0 else _load_slabs(hp_ref, 0)
            nxt = _load_slabs(h_ref, c + 1) if c + 1 < n_conv else _load_slabs(hn_ref, 0)
            outs = _conv_chunk(x, prev, nxt, cw_ref, cb_ref[...], pos > 0,
                               pos + CHUNK < geom.seq_len(row0 + c * CHUNK))
            for j, o in enumerate(outs):
                conv_ref[pl.ds(c * CHUNK + j * SUBLANES, SUBLANES), :] = o

        rs = pl.ds(r0, PROJ_SUB_ROWS)
        y = jnp.dot(m_ref[rs, :], wb_ref[:, pl.ds(n0, PROJ_SUB_COLS)], preferred_element_type=F32)
        y = epilogue(y) if mode != "forget" else lb[:, n0:n0 + PROJ_SUB_COLS] + (
            1.0 - lb[:, n0:n0 + PROJ_SUB_COLS]) * _sigmoid(y)
        for hh in range(PROJ_SUB_COLS // LANES):
            o_ref[0, n0 // LANES + hh, rs, :] = y[:, hh * LANES:(hh + 1) * LANES].astype(o_ref.dtype)


def _proj_p(geom, m, w_in, hgrn_lb, h, conv_w, conv_b, layer, tm, mode):
    d = m.shape[-1]
    n_heads = d // LANES
    blks, first_col = _PROJ_MODES[mode]
    n_kind = len(blks)
    conv_k = conv_w.shape[1]
    assert conv_k // 2 < SLABS
    if mode == "forget":
        assert first_col == n_heads - 1
        conv_split, conv_cols, conv_rows = n_kind, 1, tm // n_kind
        lane_col = lambda k: first_col
        row_blk = lambda k, i: i * conv_split + k
        out_col = lambda k: 0
    else:
        conv_split, conv_cols, conv_rows = 1, n_kind, tm
        lane_col = lambda k: first_col + k
        row_blk = lambda k, i: i
        out_col = lambda k: k
    cpr = conv_rows // CHUNK
    n_chunks = geom.t // CHUNK
    prev_chunk = lambda k, i: jnp.maximum(row_blk(k, i) * cpr - 1, 0)
    next_chunk = lambda k, i: jnp.minimum((row_blk(k, i) + 1) * cpr, n_chunks - 1)
    return pl.pallas_call(
        functools.partial(_proj_p_kernel, geom=geom, layer=layer, n_heads=n_heads, mode=mode,
                          conv_rows=conv_rows, conv_split=conv_split),
        grid=(n_kind, geom.t // tm),
        in_specs=[
            pl.BlockSpec((tm, d), lambda k, i: (i, 0)),
            pl.BlockSpec((None, d, d), lambda k, i: (layer, 0, _blk_lookup(blks, k))),
            _const_spec(hgrn_lb.shape),
            pl.BlockSpec((None, conv_rows, LANES), lambda k, i: (lane_col(k), row_blk(k, i), 0)),
            pl.BlockSpec((None, CHUNK, LANES), lambda k, i: (lane_col(k), prev_chunk(k, i), 0)),
            pl.BlockSpec((None, CHUNK, LANES), lambda k, i: (lane_col(k), next_chunk(k, i), 0)),
            pl.BlockSpec((None, conv_k, LANES), lambda k, i: (layer, 0, lane_col(k))),
            pl.BlockSpec((None, 1, LANES), lambda k, i: (layer, 0, lane_col(k))),
        ],
        out_specs=[pl.BlockSpec((1, n_heads, tm, LANES), lambda k, i: (k, 0, i, 0)),
                   pl.BlockSpec((None, conv_rows, LANES), lambda k, i: (out_col(k), row_blk(k, i), 0))],
        out_shape=[jax.ShapeDtypeStruct((n_kind, n_heads, geom.t, LANES), F32 if mode == "forget" else BF16),
                   jax.ShapeDtypeStruct((conv_cols, geom.t, LANES), F32)],
        scratch_shapes=[pltpu.VMEM((d, d), BF16)],
        compiler_params=_cparams("arbitrary", "arbitrary"),
        name=f"proj_{mode}{layer}",
    )(m, w_in, hgrn_lb, h, h, h, conv_w, conv_b)


def _level_table():
    tok = _slab_tokens()
    t, s = tok[:, None], tok[None, :]
    x = t ^ s
    lvl = np.zeros((CHUNK, CHUNK), np.int32)
    for k in range(N_LEVELS):
        lvl = np.where((x >> k) > 0, k + 1, lvl)
    lvl = np.where(s > t, -1, lvl)
    return lvl.astype(np.int32)


def _row_bcast(x, r):
    return jnp.broadcast_to(x[r:r + 1, :], x.shape)


def _sublane_gather(src, rows_from, riota):
    out = jnp.ones_like(src)
    for r, rf in enumerate(rows_from):
        if rf is not None:
            out = jnp.where(riota == r, _row_bcast(src, rf), out)
    return out


def _cat_bf16(slabs):
    return jnp.concatenate(slabs, axis=0).astype(BF16)


class _HeadChunk:
    def __init__(self, q, f, lvl, reverse):
        self.q, self.lvl, self.reverse = q, lvl, reverse
        self.kk = [1.0 - fj for fj in f]
        self.tgt = list(f)
        self.src = [jnp.ones_like(fj) for fj in f]
        diag = jnp.sum(jnp.concatenate([q[j] * self.kk[j] for j in range(SLABS)], axis=0), axis=-1, keepdims=True)
        self.a = jnp.where(lvl == 0, diag, 0.0)

    def level(self, k):
        q, kk_vec, tgt, src, reverse = self.q, self.kk, self.tgt, self.src, self.reverse
        riota = lax.broadcasted_iota(jnp.int32, (SUBLANES, LANES), 0)
        p = lax.dot_general(_cat_bf16([q[j] * tgt[j] for j in range(SLABS)]),
                            _cat_bf16([kk_vec[j] * src[j] for j in range(SLABS)]),
                            (((1,), (1,)), ((), ())), preferred_element_type=F32)
        self.a = jnp.where(self.lvl == k + 1, p, self.a)
        if k < 4:
            bit, low = 1 << k, (1 << k) - 1
            new_tgt, new_src = list(tgt), list(src)
            for j in range(SLABS):
                in_far_half = bool(j & bit) != reverse
                if in_far_half:
                    sib_edge = ((j & ~bit) | low) if not reverse else ((j | bit) & ~low)
                    new_tgt[j] = tgt[j] * tgt[sib_edge]
                else:
                    sib_edge = ((j | bit) | low) if not reverse else ((j & ~bit) & ~low)
                    new_src[j] = src[j] * tgt[sib_edge]
            self.tgt, self.src = new_tgt, new_src
        else:
            bit, low = 1 << (k - 4), (1 << (k - 4)) - 1
            edge = tgt[SLABS - 1] if not reverse else tgt[0]
            rows_t, rows_s = [], []
            for r in range(SUBLANES):
                in_far_half = bool(r & bit) != reverse
                if in_far_half:
                    rows_t.append(((r & ~bit) | low) if not reverse else ((r | bit) & ~low))
                    rows_s.append(None)
                else:
                    rows_t.append(None)
                    rows_s.append(((r | bit) | low) if not reverse else ((r & ~bit) & ~low))
            fac_t = _sublane_gather(edge, rows_t, riota)
            fac_s = _sublane_gather(edge, rows_s, riota)
            self.tgt = [x * fac_t for x in tgt]
            self.src = [x * fac_s for x in src]

    def finish(self, v_cat, st):
        q, kk_vec, tgt, src, reverse = self.q, self.kk, self.tgt, self.src, self.reverse
        lhs = jnp.concatenate([self.a.astype(BF16), _cat_bf16([q[j] * tgt[j] for j in range(SLABS)])], axis=1)
        rhs = jnp.concatenate([v_cat, st.astype(BF16)], axis=0)
        o = jnp.dot(lhs, rhs, preferred_element_type=F32)
        edge = tgt[SLABS - 1] if not reverse else tgt[0]
        row = SUBLANES - 1 if not reverse else 0
        total = jnp.transpose(jnp.broadcast_to(edge[row:row + 1, :], (LANES, LANES)))
        upd = lax.dot_general(_cat_bf16([kk_vec[j] * src[j] for j in range(SLABS)]), v_cat,
                              (((0,), (0,)), ((), ())), preferred_element_type=F32)
        return o, st * total + upd


def _hgrn_kernel(*refs, geom, reverse, n_heads, n_sub):
    if reverse:
        q_ref, v_ref, sg_ref, f_ref, of_ref, ng_ref, lvl_ref, s0_ref, o_ref, sf_ref, st_ref = refs
    else:
        q_ref, v_ref, f_ref, lvl_ref, s0_ref, o_ref, sf_ref, st_ref = refs
    rows = n_sub * CHUNK
    c = pl.program_id(0)
    n_steps = geom.t // rows
    ce = (n_steps - 1 - c) if reverse else c
    row0 = ce * rows
    pos = geom.pos_in_seq(row0)
    at_start = pos == 0
    at_end = pos == geom.seq_len(row0) - rows
    is_first, is_last = (at_end, at_start) if reverse else (at_start, at_end)

    is_ctx = row0 < geom.t_ctx

    @pl.when(is_first)
    def _():
        st_ref[...] = jnp.where(is_ctx, 0.0, s0_ref[...])

    lvl = lvl_ref[...]
    slab = lambda x: [x[j * SUBLANES:(j + 1) * SUBLANES, :] for j in range(SLABS)]
    for cc in (reversed(range(n_sub)) if reverse else range(n_sub)):
        rs = pl.ds(cc * CHUNK, CHUNK)
        for h0 in range(0, n_heads, HGRN_GROUP):
            heads = range(h0, min(h0 + HGRN_GROUP, n_heads))
            work = [_HeadChunk(slab(q_ref[0, h, rs, :].astype(F32)), slab(f_ref[0, h, rs, :]), lvl, reverse)
                    for h in heads]
            for k in range(N_LEVELS):
                for w in work:
                    w.level(k)
            for h, w in zip(heads, work):
                o, st_new = w.finish(v_ref[0, h, rs, :], st_ref[h])
                st_ref[h] = st_new
                if reverse:
                    o = o + of_ref[h, rs, :]
                    o = o * lax.rsqrt(jnp.mean(o * o, axis=-1, keepdims=True) + LN_EPS)
                    cols = pl.ds(h * LANES, LANES)
                    o_ref[rs, cols] = (o * ng_ref[:, cols] * sg_ref[0, h, rs, :].astype(F32)).astype(BF16)
                else:
                    o_ref[h, rs, :] = o

    @pl.when(jnp.logical_and(is_last, is_ctx))
    def _():
        sf_ref[0] = st_ref[...]


def _hgrn(geom, p_id, p_silu, p_forget, state_hgrn, lvl, *, reverse, layer, n_sub, o_fwd=None, norm_g=None):
    _, n_heads, t, _ = p_id.shape
    d = n_heads * LANES
    rows = n_sub * CHUNK
    n_steps = t // rows

    def step(c):
        return (n_steps - 1 - c) if reverse else c

    def blk(kind):
        return pl.BlockSpec((1, n_heads, rows, LANES), lambda c: (kind, 0, step(c), 0))

    head_rows = pl.BlockSpec((n_heads, rows, LANES), lambda c: (0, step(c), 0))
    seq = lambda c: geom.seq_of(step(c) * rows)
    direction = 1 if reverse else 0
    s0_spec = pl.BlockSpec((None, None, None, n_heads, LANES, LANES),
                           lambda c: (jnp.maximum(seq(c) - geom.n_ctx_seq, 0), layer, direction, 0, 0, 0))
    state_spec = pl.BlockSpec((1, n_heads, LANES, LANES), lambda c: (jnp.minimum(seq(c), geom.n_ctx_seq - 1), 0, 0, 0))
    lvl_spec = _const_spec((CHUNK, CHUNK))
    state_shape = jax.ShapeDtypeStruct((geom.n_ctx_seq, n_heads, LANES, LANES), F32)
    if reverse:
        in_specs = [blk(_ID_Q), blk(_ID_I), blk(0), blk(_F_BWD), head_rows, _layer_spec((1, d), layer),
                    lvl_spec, s0_spec]
        args = (p_id, p_id, p_silu, p_forget, o_fwd, norm_g, lvl, state_hgrn)
        out_specs = [pl.BlockSpec((rows, d), lambda c: (step(c), 0)), state_spec]
        out_shape = [jax.ShapeDtypeStruct((t, d), BF16), state_shape]
    else:
        in_specs = [blk(_ID_Q), blk(_ID_I), blk(_F_FWD), lvl_spec, s0_spec]
        args = (p_id, p_id, p_forget, lvl, state_hgrn)
        out_specs = [head_rows, state_spec]
        out_shape = [jax.ShapeDtypeStruct((n_heads, t, LANES), F32), state_shape]
    return pl.pallas_call(
        functools.partial(_hgrn_kernel, geom=geom, reverse=reverse, n_heads=n_heads, n_sub=n_sub),
        grid=(n_steps,),
        in_specs=in_specs,
        out_specs=out_specs,
        out_shape=out_shape,
        scratch_shapes=[pltpu.VMEM((n_heads, LANES, LANES), F32)],
        compiler_params=_cparams("arbitrary"),
        name=f"hgrn{layer}{'b' if reverse else 'f'}",
    )(*args)


def _mix_kernel(c0_ref, c1_ref, c2_ref, c3_ref, u_ref, vn_ref, ga_ref, gb_ref, gc_ref, ob_ref, x_ref, mod_ref,
                clg_ref, clb_ref, wa_ref, wb_ref, sw_ref, sb_ref, wc_ref, wo_ref, l1g_ref, l1b_ref,
                x1_ref, m2_ref, wab_ref, wbb_ref, wcb_ref, wob_ref, *, geom, tm, d, alpha):
    i = pl.program_id(0)
    cid = geom.cond_row(i * tm)
    n_heads = d // LANES

    @pl.when(i == 0)
    def _():
        wab_ref[...] = wa_ref[...].astype(BF16)
        wbb_ref[...] = wb_ref[...].astype(BF16)
        wcb_ref[...] = wc_ref[...].astype(BF16)
        wob_ref[...] = wo_ref[...].astype(BF16)

    def tokens(ref):
        return jnp.concatenate([ref[0, h] for h in range(n_heads)], axis=-1).astype(F32)

    mix = tokens(gb_ref) * jnp.dot(ob_ref[...], wbb_ref[...], preferred_element_type=F32)

    vn = vn_ref[...]
    rows = []
    for ch in range(tm // CHUNK):
        cols = []
        for g in range(n_heads):
            vg = vn[ch * CHUNK:(ch + 1) * CHUNK, g * LANES:(g + 1) * LANES]
            cols.append(jnp.dot(sw_ref[g], vg, preferred_element_type=F32) + sb_ref[g])
        rows.append(jnp.concatenate(cols, axis=-1))
    mixed = jnp.concatenate(rows, axis=0)
    yc = jnp.dot((tokens(u_ref) * mixed).astype(BF16), wcb_ref[...], preferred_element_type=F32)
    mix += tokens(gc_ref) * yc

    conv = jnp.concatenate([r[c] for r in (c0_ref, c1_ref, c2_ref, c3_ref) for c in range(r.shape[0])], axis=-1)
    ha = _silu(_layernorm(conv, clg_ref[...], clb_ref[...]))
    mix += tokens(ga_ref) * jnp.dot(ha.astype(BF16), wab_ref[...], preferred_element_type=F32)

    y = jnp.dot(mix.astype(BF16), wob_ref[...], preferred_element_type=F32)
    g1, sh2, sc2 = _mod_rows(mod_ref, cid, d, (2, 3, 4))
    x1 = _layernorm(alpha * x_ref[...] + g1 * y, l1g_ref[...], l1b_ref[...])
    x1_ref[...] = x1
    m2_ref[...] = (x1 * (1.0 + sc2) + sh2).astype(BF16)


def _mix(geom, convs, p_id, p_sig, vn, ob, x, mod, p, sgu_w, sgu_b, layer, tm, alpha):
    t, d = x.shape
    n_heads = d // LANES
    n_g = sgu_w.shape[0]
    assert tm % CHUNK == 0 and n_g * LANES == d and sum(c.shape[0] for c in convs) == n_heads

    def kind(k):
        return pl.BlockSpec((1, n_heads, tm, LANES), lambda i: (k, 0, i, 0))

    tok = pl.BlockSpec((tm, d), lambda i: (i, 0))
    vec = _layer_spec((1, d), layer)
    mat = _layer_spec((d, d), layer)
    in_specs = [pl.BlockSpec((c.shape[0], tm, LANES), lambda i: (0, i, 0)) for c in convs]
    in_specs += [kind(_ID_U), tok, kind(0), kind(1), kind(2), tok, tok,
                _layer_spec(mod.shape[1:], layer),
                vec, vec, mat, mat,
                _const_spec((n_g, CHUNK, CHUNK)), _const_spec((n_g, CHUNK, 1)), mat, mat, vec, vec]
    return pl.pallas_call(
        functools.partial(_mix_kernel, geom=geom, tm=tm, d=d, alpha=alpha),
        grid=(t // tm,),
        in_specs=in_specs,
        out_specs=[tok, tok],
        out_shape=[jax.ShapeDtypeStruct((t, d), F32), jax.ShapeDtypeStruct((t, d), BF16)],
        scratch_shapes=[pltpu.VMEM((d, d), BF16)] * 4,
        compiler_params=_cparams("arbitrary"),
        name=f"mix{layer}",
    )(*convs, p_id, vn, p_sig, p_sig, p_sig, ob, x, mod,
      p["conv_ln_g"], p["conv_ln_b"], p["w_a_out"], p["w_b_out"],
      sgu_w, sgu_b, p["w_c_out"], p["w_o"], p["ln1_g"], p["ln1_b"])


def _ffn_kernel(m2_ref, x1_ref, mod_ref, modn_ref, wi_ref, wo_ref, g_ref, b_ref, x2_ref, *mn_ref,
                geom, tm, d, f_hidden, f_chunk, alpha):
    i = pl.program_id(0)
    cid = geom.cond_row(i * tm)
    (g2,) = _mod_rows(mod_ref, cid, d, (5,))
    for r0 in range(0, tm, PROJ_SUB_ROWS):
        rs = pl.ds(r0, PROJ_SUB_ROWS)
        m2 = m2_ref[rs, :]
        acc = jnp.zeros((PROJ_SUB_ROWS, d), F32)
        for c0 in range(0, f_hidden, f_chunk):
            gte = jnp.dot(m2, wi_ref[:, pl.ds(c0, f_chunk)], preferred_element_type=F32)
            up = jnp.dot(m2, wi_ref[:, pl.ds(f_hidden + c0, f_chunk)], preferred_element_type=F32)
            acc += jnp.dot((_silu(gte) * up).astype(BF16), wo_ref[pl.ds(c0, f_chunk), :],
                           preferred_element_type=F32)
        x2 = _layernorm(alpha * x1_ref[rs, :] + g2 * acc, g_ref[...], b_ref[...])
        x2_ref[rs, :] = x2
        if mn_ref:
            sh1, sc1 = _mod_rows(modn_ref, cid, d, (0, 1))
            mn_ref[0][rs, :] = (x2 * (1.0 + sc1) + sh1).astype(BF16)


def _ffn(geom, m2, x1, mod, w_ffn_in_bf16, w_ffn_out_bf16, ln2_g, ln2_b, layer, tm, alpha):
    t, d = x1.shape
    n_layers, f_hidden, _ = w_ffn_out_bf16.shape
    f_chunk = f_hidden // 2
    assert f_chunk % LANES == 0
    emit_next = layer + 1 < n_layers
    tok = pl.BlockSpec((tm, d), lambda i: (i, 0))
    out_shape = [jax.ShapeDtypeStruct((t, d), F32)] + ([jax.ShapeDtypeStruct((t, d), BF16)] if emit_next else [])
    return pl.pallas_call(
        functools.partial(_ffn_kernel, geom=geom, tm=tm, d=d, f_hidden=f_hidden, f_chunk=f_chunk, alpha=alpha),
        grid=(t // tm,),
        in_specs=[tok, tok, _layer_spec(mod.shape[1:], layer),
                  _layer_spec(mod.shape[1:], layer + 1 if emit_next else layer),
                  _layer_spec((d, 2 * f_hidden), layer), _layer_spec((f_hidden, d), layer),
                  _layer_spec((1, d), layer), _layer_spec((1, d), layer)],
        out_specs=[tok] * len(out_shape),
        out_shape=out_shape,
        compiler_params=_cparams("arbitrary"),
        name=f"ffn{layer}",
    )(m2, x1, mod, mod, w_ffn_in_bf16, w_ffn_out_bf16, ln2_g, ln2_b)


def _sincos_2d(n_tokens, dim):
    rows = n_tokens // GRID_W
    t = jnp.arange(rows * GRID_W)
    r = (t // GRID_W).astype(F32)
    col = (t % GRID_W).astype(F32)
    nf = dim // 4
    omega = 1.0 / (10000.0 ** (jnp.arange(nf, dtype=F32) / nf))
    ar = r[:, None] * omega
    ac = col[:, None] * omega
    return jnp.concatenate([jnp.sin(ar), jnp.cos(ar), jnp.sin(ac), jnp.cos(ac)], axis=-1)


def kernel(x_prompt, x_sample, state_hgrn, c, c_ctx, ln_in_g, ln_in_b, w_ada, b_ada, w_in, conv_w, conv_b, conv_ln_g, conv_ln_b, w_a_out, hgrn_lb, hgrn_norm_g, w_b_out, sgu_ln_g, sgu_ln_b, sgu_w, sgu_b, w_c_out, w_o, ln1_g, ln1_b, w_ffn_in, w_ffn_out, ln2_g, ln2_b):
    n_ctx_seq, ctx_len, d = x_prompt.shape
    n_lat_seq, lat_len, _ = x_sample.shape
    n_layers = w_in.shape[0]
    n_heads = state_hgrn.shape[3]
    geom = _Geom(n_ctx_seq, ctx_len, n_lat_seq, lat_len)
    alpha = float((2 * n_layers) ** 0.25)
    assert d == n_heads * LANES and w_in.shape[2] == N_BLK * d
    assert 1 + n_lat_seq <= COND_ROWS and ctx_len % CHUNK == 0 and lat_len % CHUNK == 0

    def row_tile(limit, within_seq):
        tm = CHUNK
        while (2 * tm <= limit and lat_len % (2 * tm) == 0
               and (ctx_len if within_seq else geom.t_ctx) % (2 * tm) == 0):
            tm *= 2
        return tm

    tm_mix = row_tile(256, False)
    tm_ffn = row_tile(512, False)
    tm_proj = row_tile(1024, False)
    n_sub = row_tile(256, True) // CHUNK

    cond = jnp.zeros((COND_ROWS, d), F32).at[0].set(c_ctx).at[1:1 + n_lat_seq].set(c)
    mod = _ada(cond, w_ada, b_ada)

    vec = lambda a: a.reshape(n_layers, 1, d)
    tok = _slab_tokens()
    pos = _to_slab_order(_sincos_2d(lat_len, d))
    x, m = _pre(geom, _to_slab_order(x_prompt.reshape(geom.t_ctx, d)), _to_slab_order(x_sample.reshape(geom.t_lat, d)),
                pos, ln_in_g.reshape(1, d), ln_in_b.reshape(1, d), mod, tm_ffn)

    lvl_f = jnp.asarray(_level_table())
    lvl_b = jnp.asarray(_level_table().T.copy())
    p = dict(conv_ln_g=vec(conv_ln_g), conv_ln_b=vec(conv_ln_b),
             w_a_out=w_a_out, w_b_out=w_b_out, w_c_out=w_c_out, w_o=w_o, ln1_g=vec(ln1_g), ln1_b=vec(ln1_b))
    conv_bv, norm_g, sgu_g, sgu_bb = vec(conv_b), vec(hgrn_norm_g), vec(sgu_ln_g), vec(sgu_ln_b)
    ln2_gv, ln2_bv = vec(ln2_g), vec(ln2_b)
    w_ffn_in_bf16, w_ffn_out_bf16 = w_ffn_in.astype(BF16), w_ffn_out.astype(BF16)
    states = []
    for l in range(n_layers):
        h, vn = _proj_a(geom, m, w_in, sgu_g, sgu_bb, l, tm_ffn)
        (p_id, c0), (p_sig, c1), (p_silu, c2), (p_forget, c3) = (
            _proj_p(geom, m, w_in, hgrn_lb, h, conv_w, conv_bv, l, tm_proj, mode) for mode in _PROJ_MODES)
        o_f, sf_f = _hgrn(geom, p_id, p_silu, p_forget, state_hgrn, lvl_f, reverse=False, layer=l, n_sub=n_sub)
        ob, sf_b = _hgrn(geom, p_id, p_silu, p_forget, state_hgrn, lvl_b, reverse=True, layer=l, n_sub=n_sub,
                         o_fwd=o_f, norm_g=norm_g)
        states.append(jnp.stack([sf_f, sf_b], axis=1))
        sw = sgu_w[l][:, tok][:, :, tok].astype(BF16)
        sb = sgu_b[l][:, tok][:, :, None]
        x1, m2 = _mix(geom, (c0, c1, c2, c3), p_id, p_sig, vn, ob, x, mod, p, sw, sb, l, tm_mix, alpha)
        out = _ffn(geom, m2, x1, mod, w_ffn_in_bf16, w_ffn_out_bf16, ln2_gv, ln2_bv, l, tm_ffn, alpha)
        x, m = (out[0], out[1]) if l + 1 < n_layers else (out[0], None)

    x = _from_slab_order(x)
    y_prompt = x[:geom.t_ctx].reshape(x_prompt.shape)
    y_sample = x[geom.t_ctx:].reshape(x_sample.shape)
    return y_prompt, y_sample, jnp.stack(states, axis=1)
```

```python
import functools

import jax
import jax.numpy as jnp
import numpy as np
from jax import lax
from jax.experimental import pallas as pl
from jax.experimental.pallas import tpu as pltpu

LN_EPS = 1e-5
GRID_W = 64
LANES = 128
SUBLANES = 8
CHUNK = 128
SLABS = CHUNK // SUBLANES
N_LEVELS = 7
COND_ROWS = 8
HGRN_GROUP = 8
PROJ_SUB_ROWS = 256
PROJ_SUB_COLS = 256
VMEM_LIMIT = 56 * 1024 * 1024

F32 = jnp.float32
BF16 = jnp.bfloat16


def _cparams(*sem):
    return pltpu.CompilerParams(dimension_semantics=sem, vmem_limit_bytes=VMEM_LIMIT)


def _sigmoid(x):
    return 1.0 / (1.0 + jnp.exp(-x))


def _silu(x):
    return x * _sigmoid(x)


def _layernorm(x, g, b):
    mu = jnp.mean(x, axis=-1, keepdims=True)
    xc = x - mu
    var = jnp.mean(xc * xc, axis=-1, keepdims=True)
    return xc * lax.rsqrt(var + LN_EPS) * g + b


def _const_spec(shape):
    nd = len(shape)
    return pl.BlockSpec(shape, lambda *_: (0,) * nd, pipeline_mode=pl.Buffered(1))


def _layer_spec(shape, layer):
    nd = len(shape)
    return pl.BlockSpec((None,) + tuple(shape), lambda *_: (layer,) + (0,) * nd, pipeline_mode=pl.Buffered(1))


def _slab_tokens():
    p = np.arange(CHUNK)
    return SLABS * (p % SUBLANES) + p // SUBLANES


def _to_slab_order(x):
    t = x.shape[0]
    rest = x.shape[1:]
    return x.reshape((t // CHUNK, SUBLANES, SLABS) + rest).swapaxes(1, 2).reshape((t,) + rest)


def _ada_kernel(cond_ref, w_ref, b_ref, o_ref):
    s = _silu(cond_ref[...]).astype(BF16)
    o_ref[0] = jnp.dot(s, w_ref[0].astype(BF16), preferred_element_type=F32) + b_ref[0]


def _ada(cond, w_ada, b_ada):
    n_layers, d, n6 = w_ada.shape
    tn = n6 // 4
    return pl.pallas_call(
        _ada_kernel,
        grid=(n_layers, n6 // tn),
        in_specs=[
            pl.BlockSpec((COND_ROWS, d), lambda l, j: (0, 0)),
            pl.BlockSpec((1, d, tn), lambda l, j: (l, 0, j)),
            pl.BlockSpec((1, 1, tn), lambda l, j: (l, 0, j)),
        ],
        out_specs=pl.BlockSpec((1, COND_ROWS, tn), lambda l, j: (l, 0, j)),
        out_shape=jax.ShapeDtypeStruct((n_layers, COND_ROWS, n6), F32),
        compiler_params=_cparams("arbitrary", "arbitrary"),
        name="ada",
    )(cond, w_ada, b_ada.reshape(n_layers, 1, n6))


class _Geom:
    def __init__(self, n_ctx_seq, ctx_len, n_lat_seq, lat_len):
        self.n_ctx_seq, self.ctx_len = n_ctx_seq, ctx_len
        self.n_lat_seq, self.lat_len = n_lat_seq, lat_len
        self.t_ctx = n_ctx_seq * ctx_len
        self.t_lat = n_lat_seq * lat_len
        self.t = self.t_ctx + self.t_lat
        self.n_seq = n_ctx_seq + n_lat_seq

    def cond_row(self, row0):
        return jnp.where(row0 < self.t_ctx, 0, 1 + (row0 - self.t_ctx) // self.lat_len)

    def seq_of(self, row0):
        return jnp.where(row0 < self.t_ctx, row0 // self.ctx_len,
                         self.n_ctx_seq + (row0 - self.t_ctx) // self.lat_len)

    def pos_in_seq(self, row0):
        return jnp.where(row0 < self.t_ctx, row0 % self.ctx_len, (row0 - self.t_ctx) % self.lat_len)

    def seq_len(self, row0):
        return jnp.where(row0 < self.t_ctx, self.ctx_len, self.lat_len)


def _mod_rows(mod_ref, cid, d, idx):
    return [mod_ref[pl.ds(cid, 1), pl.ds(i * d, d)] for i in idx]


def _row_permutation(to_slab_order):
    tok = _slab_tokens()
    p = np.zeros((CHUNK, CHUNK), np.float32)
    p[np.arange(CHUNK), tok] = 1.0
    if not to_slab_order:
        p = p.T
    return jnp.asarray(np.concatenate([p, p, p], axis=1), dtype=BF16)


def _permute_rows(x, perm3):
    out = []
    for c in range(x.shape[0] // CHUNK):
        xc = x[c * CHUNK:(c + 1) * CHUNK, :]
        hi = xc.astype(BF16)
        r1 = xc - hi.astype(F32)
        mid = r1.astype(BF16)
        lo = (r1 - mid.astype(F32)).astype(BF16)
        out.append(jnp.dot(perm3, jnp.concatenate([hi, mid, lo], axis=0), preferred_element_type=F32))
    return jnp.concatenate(out, axis=0)


def _pre_kernel(xc_ref, xl_ref, pos_ref, perm_ref, g_ref, b_ref, mod_ref, x_ref, m_ref, *, geom, tm, d):
    i = pl.program_id(0)
    row0 = i * tm
    cid = geom.cond_row(row0)
    sh1, sc1 = _mod_rows(mod_ref, cid, d, (0, 1))

    def emit(x):
        xn = _layernorm(x, g_ref[...], b_ref[...])
        x_ref[...] = xn
        m_ref[...] = (xn * (1.0 + sc1) + sh1).astype(BF16)

    @pl.when(row0 < geom.t_ctx)
    def _():
        emit(_permute_rows(xc_ref[...], perm_ref[...]))

    @pl.when(row0 >= geom.t_ctx)
    def _():
        emit(_permute_rows(xl_ref[...], perm_ref[...]) + pos_ref[...])


def _pre(geom, x_ctx, x_lat, pos, g, b, mod, tm):
    d = x_ctx.shape[-1]
    n_ctx_tiles = geom.t_ctx // tm
    pos_tiles = geom.lat_len // tm
    return pl.pallas_call(
        functools.partial(_pre_kernel, geom=geom, tm=tm, d=d),
        grid=(geom.t // tm,),
        in_specs=[
            pl.BlockSpec((tm, d), lambda i: (jnp.minimum(i, n_ctx_tiles - 1), 0)),
            pl.BlockSpec((tm, d), lambda i: (jnp.maximum(i - n_ctx_tiles, 0), 0)),
            pl.BlockSpec((tm, d), lambda i: (jnp.maximum(i - n_ctx_tiles, 0) % pos_tiles, 0)),
            _const_spec((CHUNK, 3 * CHUNK)),
            _const_spec((1, d)), _const_spec((1, d)), _layer_spec(mod.shape[1:], 0),
        ],
        out_specs=[pl.BlockSpec((tm, d), lambda i: (i, 0))] * 2,
        out_shape=[jax.ShapeDtypeStruct((geom.t, d), F32), jax.ShapeDtypeStruct((geom.t, d), BF16)],
        compiler_params=_cparams("arbitrary"),
        name="pre",
    )(x_ctx, x_lat, pos, _row_permutation(True), g, b, mod)


_W_VAL, _W_GATE, _W_Q, _W_ZF, _W_ZB, _W_I, _W_G, _W_U, _W_V, _W_GA, _W_GB, _W_GC = range(12)
N_BLK = 12
_PROJ_MODES = {
    "id": ((_W_U, _W_Q, _W_I), 0),
    "sig": ((_W_GA, _W_GB, _W_GC), 3),
    "silu": ((_W_G,), 6),
    "forget": ((_W_ZF, _W_ZB), 7),
}
_ID_U, _ID_Q, _ID_I = range(3)
_F_FWD, _F_BWD = range(2)


def _blk_lookup(blks, s):
    out = blks[-1]
    for k in range(len(blks) - 2, -1, -1):
        out = jnp.where(s == k, blks[k], out)
    return out


def _proj_a_kernel(m_ref, wv_ref, wg_ref, ws_ref, g_ref, b_ref, h_ref, vn_ref, wvb_ref, wgb_ref, wsb_ref):
    @pl.when(pl.program_id(0) == 0)
    def _():
        wvb_ref[...] = wv_ref[...].astype(BF16)
        wgb_ref[...] = wg_ref[...].astype(BF16)
        wsb_ref[...] = ws_ref[...].astype(BF16)

    for r0 in range(0, m_ref.shape[0], PROJ_SUB_ROWS):
        rs = pl.ds(r0, PROJ_SUB_ROWS)
        m = m_ref[rs, :]
        val = jnp.dot(m, wvb_ref[...], preferred_element_type=F32)
        gate = jnp.dot(m, wgb_ref[...], preferred_element_type=F32)
        hv = (val * _sigmoid(gate)).astype(BF16)
        for hh in range(h_ref.shape[0]):
            h_ref[hh, rs, :] = hv[:, hh * LANES:(hh + 1) * LANES]
        v = jnp.dot(m, wsb_ref[...], preferred_element_type=F32)
        vn_ref[rs, :] = _layernorm(v, g_ref[...], b_ref[...]).astype(BF16)


def _proj_a(geom, m, w_in, sgu_g, sgu_b, layer, tm):
    d = m.shape[-1]
    tok = pl.BlockSpec((tm, d), lambda i: (i, 0))

    def wblk(b):
        return pl.BlockSpec((None, d, d), lambda i: (layer, 0, b), pipeline_mode=pl.Buffered(1))

    return pl.pallas_call(
        _proj_a_kernel,
        grid=(geom.t // tm,),
        in_specs=[tok, wblk(_W_VAL), wblk(_W_GATE), wblk(_W_V), _layer_spec((1, d), layer), _layer_spec((1, d), layer)],
        out_specs=[pl.BlockSpec((d // LANES, tm, LANES), lambda i: (0, i, 0)), tok],
        out_shape=[jax.ShapeDtypeStruct((d // LANES, geom.t, LANES), BF16), jax.ShapeDtypeStruct((geom.t, d), BF16)],
        scratch_shapes=[pltpu.VMEM((d, d), BF16)] * 3,
        compiler_params=_cparams("arbitrary"),
        name=f"proj_a{layer}",
    )(m, w_in, w_in, w_in, sgu_g, sgu_b)


def _load_slabs(ref, c):
    out = []
    for p in range(SLABS // 2):
        x = ref[pl.ds(c * CHUNK + 2 * p * SUBLANES, 2 * SUBLANES), :].astype(F32)
        out += [x[:SUBLANES], x[SUBLANES:]]
    return out


def _conv_chunk(x, prev, nxt, w_ref, bias, has_prev, has_next):
    conv_k = w_ref.shape[0]
    half = conv_k // 2
    riota = lax.broadcasted_iota(jnp.int32, (SUBLANES, LANES), 0)
    up, down = [], []
    for j in range(SLABS):
        first_next = jnp.where(has_next, nxt[j][0:1, :], 0.0)
        last_prev = jnp.where(has_prev, prev[j][SUBLANES - 1:SUBLANES, :], 0.0)
        up.append(jnp.where(riota == SUBLANES - 1, first_next, pltpu.roll(x[j], SUBLANES - 1, 0)))
        down.append(jnp.where(riota == 0, last_prev, pltpu.roll(x[j], 1, 0)))
    group = SLABS // 2
    outs = []
    for j0 in range(0, SLABS, group):
        accs = [bias] * group
        for k in range(conv_k):
            wk = jnp.broadcast_to(w_ref[k:k + 1, :], (SUBLANES, LANES))
            for jj in range(group):
                s = j0 + jj + k - half
                term = x[s] if 0 <= s < SLABS else (up[s - SLABS] if s >= SLABS else down[s + SLABS])
                accs[jj] = accs[jj] + term * wk
        outs += accs
    return outs


def _proj_p_kernel(m_ref, w_ref, lb_ref, h_ref, hp_ref, hn_ref, cw_ref, cb_ref, o_ref, conv_ref, wb_ref,
                   *, geom, layer, n_heads, mode, conv_rows, conv_split):
    k = pl.program_id(0)
    i = pl.program_id(1)

    @pl.when(i == 0)
    def _():
        wb_ref[...] = w_ref[...].astype(BF16)

    if mode == "forget":
        lb_all = lb_ref[...]
        e = jnp.exp(lb_all - jnp.max(lb_all, axis=0, keepdims=True))
        soft = e / jnp.sum(e, axis=0, keepdims=True)
        lb = jnp.sum(soft[:layer + 1], axis=0, keepdims=True) - soft[0:1]
    epilogue = {"id": lambda a: a, "sig": _sigmoid, "silu": _silu, "forget": None}[mode]

    tm, d = m_ref.shape
    n_conv = conv_rows // CHUNK
    row0 = (i * conv_split + k) * conv_rows if conv_split > 1 else i * conv_rows
    pieces = [(r0, n0) for r0 in range(0, tm, PROJ_SUB_ROWS) for n0 in range(0, d, PROJ_SUB_COLS)]
    conv_before = {c * len(pieces) // n_conv: c for c in range(n_conv)}
    for p, (r0, n0) in enumerate(pieces):
        if p in conv_before:
            c = conv_before[p]
            pos = geom.pos_in_seq(row0 + c * CHUNK)
            x = _load_slabs(h_ref, c)
            prev = _load_slabs(h_ref, c - 1) if c > 0 else _load_slabs(hp_ref, 0)
            nxt = _load_slabs(h_ref, c + 1) if c + 1 < n_conv else _load_slabs(hn_ref, 0)
            outs = _conv_chunk(x, prev, nxt, cw_ref, cb_ref[...], pos > 0,
                               pos + CHUNK < geom.seq_len(row0 + c * CHUNK))
            for j, o in enumerate(outs):
                conv_ref[pl.ds(c * CHUNK + j * SUBLANES, SUBLANES), :] = o

        rs = pl.ds(r0, PROJ_SUB_ROWS)
        y = jnp.dot(m_ref[rs, :], wb_ref[:, pl.ds(n0, PROJ_SUB_COLS)], preferred_element_type=F32)
        y = epilogue(y) if mode != "forget" else lb[:, n0:n0 + PROJ_SUB_COLS] + (
            1.0 - lb[:, n0:n0 + PROJ_SUB_COLS]) * _sigmoid(y)
        for hh in range(PROJ_SUB_COLS // LANES):
            o_ref[0, n0 // LANES + hh, rs, :] = y[:, hh * LANES:(hh + 1) * LANES].astype(o_ref.dtype)


def _proj_p(geom, m, w_in, hgrn_lb, h, conv_w, conv_b, layer, tm, mode):
    d = m.shape[-1]
    n_heads = d // LANES
    blks, first_col = _PROJ_MODES[mode]
    n_kind = len(blks)
    conv_k = conv_w.shape[1]
    assert conv_k // 2 < SLABS
    if mode == "forget":
        assert first_col == n_heads - 1
        conv_split, conv_cols, conv_rows = n_kind, 1, tm // n_kind
        lane_col = lambda k: first_col
        row_blk = lambda k, i: i * conv_split + k
        out_col = lambda k: 0
    else:
        conv_split, conv_cols, conv_rows = 1, n_kind, tm
        lane_col = lambda k: first_col + k
        row_blk = lambda k, i: i
        out_col = lambda k: k
    cpr = conv_rows // CHUNK
    n_chunks = geom.t // CHUNK
    prev_chunk = lambda k, i: jnp.maximum(row_blk(k, i) * cpr - 1, 0)
    next_chunk = lambda k, i: jnp.minimum((row_blk(k, i) + 1) * cpr, n_chunks - 1)
    return pl.pallas_call(
        functools.partial(_proj_p_kernel, geom=geom, layer=layer, n_heads=n_heads, mode=mode,
                          conv_rows=conv_rows, conv_split=conv_split),
        grid=(n_kind, geom.t // tm),
        in_specs=[
            pl.BlockSpec((tm, d), lambda k, i: (i, 0)),
            pl.BlockSpec((None, d, d), lambda k, i: (layer, 0, _blk_lookup(blks, k))),
            _const_spec(hgrn_lb.shape),
            pl.BlockSpec((None, conv_rows, LANES), lambda k, i: (lane_col(k), row_blk(k, i), 0)),
            pl.BlockSpec((None, CHUNK, LANES), lambda k, i: (lane_col(k), prev_chunk(k, i), 0)),
            pl.BlockSpec((None, CHUNK, LANES), lambda k, i: (lane_col(k), next_chunk(k, i), 0)),
            pl.BlockSpec((None, conv_k, LANES), lambda k, i: (layer, 0, lane_col(k))),
            pl.BlockSpec((None, 1, LANES), lambda k, i: (layer, 0, lane_col(k))),
        ],
        out_specs=[pl.BlockSpec((1, n_heads, tm, LANES), lambda k, i: (k, 0, i, 0)),
                   pl.BlockSpec((None, conv_rows, LANES), lambda k, i: (out_col(k), row_blk(k, i), 0))],
        out_shape=[jax.ShapeDtypeStruct((n_kind, n_heads, geom.t, LANES), F32 if mode == "forget" else BF16),
                   jax.ShapeDtypeStruct((conv_cols, geom.t, LANES), F32)],
        scratch_shapes=[pltpu.VMEM((d, d), BF16)],
        compiler_params=_cparams("arbitrary", "arbitrary"),
        name=f"proj_{mode}{layer}",
    )(m, w_in, hgrn_lb, h, h, h, conv_w, conv_b)


def _level_table():
    tok = _slab_tokens()
    t, s = tok[:, None], tok[None, :]
    x = t ^ s
    lvl = np.zeros((CHUNK, CHUNK), np.int32)
    for k in range(N_LEVELS):
        lvl = np.where((x >> k) > 0, k + 1, lvl)
    lvl = np.where(s > t, -1, lvl)
    return lvl.astype(np.int32)


def _row_bcast(x, r):
    return jnp.broadcast_to(x[r:r + 1, :], x.shape)


def _sublane_gather(src, rows_from, riota):
    out = jnp.ones_like(src)
    for r, rf in enumerate(rows_from):
        if rf is not None:
            out = jnp.where(riota == r, _row_bcast(src, rf), out)
    return out


def _cat_bf16(slabs):
    return jnp.concatenate(slabs, axis=0).astype(BF16)


class _HeadChunk:
    def __init__(self, q, f, lvl, reverse):
        self.lvl, self.reverse = lvl, reverse
        kk = [1.0 - fj for fj in f]
        self.tgt = list(f)
        self.qt = [q[j] * f[j] for j in range(SLABS)]
        self.ks = kk
        self.edge_slab = 0 if reverse else SLABS - 1
        diag = jnp.sum(jnp.concatenate([q[j] * kk[j] for j in range(SLABS)], axis=0), axis=-1, keepdims=True)
        self.a = jnp.where(lvl == 0, diag, 0.0)

    def level(self, k):
        tgt, reverse = self.tgt, self.reverse
        p = lax.dot_general(_cat_bf16(self.qt), _cat_bf16(self.ks), (((1,), (1,)), ((), ())),
                            preferred_element_type=F32)
        self.a = jnp.where(self.lvl == k + 1, p, self.a)
        if k < 4:
            bit, low = 1 << k, (1 << k) - 1
            new_tgt, new_qt, new_ks = list(tgt), list(self.qt), list(self.ks)
            for j in range(SLABS):
                in_far_half = bool(j & bit) != reverse
                if in_far_half:
                    sib_edge = ((j & ~bit) | low) if not reverse else ((j | bit) & ~low)
                    new_qt[j] = self.qt[j] * tgt[sib_edge]
                    below = j & (2 * bit - 1)
                    if below == (0 if reverse else 2 * bit - 1):
                        new_tgt[j] = tgt[j] * tgt[sib_edge]
                else:
                    sib_edge = ((j | bit) | low) if not reverse else ((j & ~bit) & ~low)
                    new_ks[j] = self.ks[j] * tgt[sib_edge]
            self.tgt, self.qt, self.ks = new_tgt, new_qt, new_ks
        else:
            bit, low = 1 << (k - 4), (1 << (k - 4)) - 1
            riota = lax.broadcasted_iota(jnp.int32, (SUBLANES, LANES), 0)
            edge = tgt[self.edge_slab]
            rows_t, rows_s = [], []
            for r in range(SUBLANES):
                in_far_half = bool(r & bit) != reverse
                if in_far_half:
                    rows_t.append(((r & ~bit) | low) if not reverse else ((r | bit) & ~low))
                    rows_s.append(None)
                else:
                    rows_t.append(None)
                    rows_s.append(((r | bit) | low) if not reverse else ((r & ~bit) & ~low))
            fac_t = _sublane_gather(edge, rows_t, riota)
            fac_s = _sublane_gather(edge, rows_s, riota)
            self.qt = [x * fac_t for x in self.qt]
            self.ks = [x * fac_s for x in self.ks]
            self.tgt = list(tgt)
            self.tgt[self.edge_slab] = edge * fac_t

    def finish(self, v_cat, st):
        lhs = jnp.concatenate([self.a.astype(BF16), _cat_bf16(self.qt)], axis=1)
        rhs = jnp.concatenate([v_cat, st.astype(BF16)], axis=0)
        o = jnp.dot(lhs, rhs, preferred_element_type=F32)
        row = 0 if self.reverse else SUBLANES - 1
        edge = self.tgt[self.edge_slab]
        total = jnp.transpose(jnp.broadcast_to(edge[row:row + 1, :], (LANES, LANES)))
        upd = lax.dot_general(_cat_bf16(self.ks), v_cat, (((0,), (0,)), ((), ())), preferred_element_type=F32)
        return o, st * total + upd


def _hgrn_kernel(*refs, geom, reverse, n_heads, n_sub, has_acc):
    *ins, o_ref, sf_ref, st_ref = refs
    if has_acc:
        ins = ins[:-1]
    if reverse:
        q_ref, v_ref, sg_ref, f_ref, of_ref, ng_ref, lvl_ref, s0_ref = ins
    else:
        q_ref, v_ref, f_ref, lvl_ref, s0_ref = ins
    rows = n_sub * CHUNK
    c = pl.program_id(0)
    n_steps = geom.t // rows
    ce = (n_steps - 1 - c) if reverse else c
    row0 = ce * rows
    pos = geom.pos_in_seq(row0)
    at_start = pos == 0
    at_end = pos == geom.seq_len(row0) - rows
    is_first, is_last = (at_end, at_start) if reverse else (at_start, at_end)

    is_ctx = row0 < geom.t_ctx

    @pl.when(is_first)
    def _():
        st_ref[...] = jnp.where(is_ctx, 0.0, s0_ref[...])

    lvl = lvl_ref[...]
    slab = lambda x: [x[j * SUBLANES:(j + 1) * SUBLANES, :] for j in range(SLABS)]
    for cc in (reversed(range(n_sub)) if reverse else range(n_sub)):
        rs = pl.ds(cc * CHUNK, CHUNK)
        for h0 in range(0, n_heads, HGRN_GROUP):
            heads = range(h0, min(h0 + HGRN_GROUP, n_heads))
            work = [_HeadChunk(slab(q_ref[0, h, rs, :].astype(F32)), slab(f_ref[0, h, rs, :]), lvl, reverse)
                    for h in heads]
            for k in range(N_LEVELS):
                for w in work:
                    w.level(k)
            for h, w in zip(heads, work):
                o, st_new = w.finish(v_ref[0, h, rs, :], st_ref[h])
                st_ref[h] = st_new
                if reverse:
                    o = o + of_ref[h, rs, :]
                    o = o * lax.rsqrt(jnp.mean(o * o, axis=-1, keepdims=True) + LN_EPS)
                    cols = pl.ds(h * LANES, LANES)
                    o_ref[rs, cols] = (o * ng_ref[:, cols] * sg_ref[0, h, rs, :].astype(F32)).astype(BF16)
                else:
                    o_ref[h, rs, :] = o

    @pl.when(jnp.logical_and(is_last, is_ctx))
    def _():
        sf_ref[0] = st_ref[...]


def _hgrn(geom, p_id, p_silu, p_forget, state_hgrn, lvl, *, reverse, layer, n_sub, o_fwd=None, norm_g=None,
          states_acc=None):
    _, n_heads, t, _ = p_id.shape
    n_layers = state_hgrn.shape[1]
    d = n_heads * LANES
    rows = n_sub * CHUNK
    n_steps = t // rows

    def step(c):
        return (n_steps - 1 - c) if reverse else c

    def blk(kind):
        return pl.BlockSpec((1, n_heads, rows, LANES), lambda c: (kind, 0, step(c), 0))

    head_rows = pl.BlockSpec((n_heads, rows, LANES), lambda c: (0, step(c), 0))
    seq = lambda c: geom.seq_of(step(c) * rows)
    direction = 1 if reverse else 0
    s0_spec = pl.BlockSpec((None, None, None, n_heads, LANES, LANES),
                           lambda c: (jnp.maximum(seq(c) - geom.n_ctx_seq, 0), layer, direction, 0, 0, 0))
    state_spec = pl.BlockSpec((1, None, None, n_heads, LANES, LANES),
                              lambda c: (jnp.minimum(seq(c), geom.n_ctx_seq - 1), layer, direction, 0, 0, 0))
    lvl_spec = _const_spec((CHUNK, CHUNK))
    state_shape = jax.ShapeDtypeStruct((geom.n_ctx_seq, n_layers, 2, n_heads, LANES, LANES), F32)
    if reverse:
        in_specs = [blk(_ID_Q), blk(_ID_I), blk(0), blk(_F_BWD), head_rows, _layer_spec((1, d), layer),
                    lvl_spec, s0_spec]
        args = (p_id, p_id, p_silu, p_forget, o_fwd, norm_g, lvl, state_hgrn)
        out_specs = [pl.BlockSpec((rows, d), lambda c: (step(c), 0)), state_spec]
        out_shape = [jax.ShapeDtypeStruct((t, d), BF16), state_shape]
    else:
        in_specs = [blk(_ID_Q), blk(_ID_I), blk(_F_FWD), lvl_spec, s0_spec]
        args = (p_id, p_id, p_forget, lvl, state_hgrn)
        out_specs = [head_rows, state_spec]
        out_shape = [jax.ShapeDtypeStruct((n_heads, t, LANES), F32), state_shape]
    aliases = {}
    if states_acc is not None:
        aliases = {len(args): 1}
        in_specs = in_specs + [pl.BlockSpec(memory_space=pl.ANY)]
        args = args + (states_acc,)
    return pl.pallas_call(
        functools.partial(_hgrn_kernel, geom=geom, reverse=reverse, n_heads=n_heads, n_sub=n_sub,
                          has_acc=states_acc is not None),
        grid=(n_steps,),
        in_specs=in_specs,
        out_specs=out_specs,
        out_shape=out_shape,
        input_output_aliases=aliases,
        scratch_shapes=[pltpu.VMEM((n_heads, LANES, LANES), F32)],
        compiler_params=_cparams("arbitrary"),
        name=f"hgrn{layer}{'b' if reverse else 'f'}",
    )(*args)


def _mix_kernel(c0_ref, c1_ref, c2_ref, c3_ref, u_ref, vn_ref, ga_ref, gb_ref, gc_ref, ob_ref, x_ref, mod_ref,
                clg_ref, clb_ref, wa_ref, wb_ref, sw_ref, sb_ref, wc_ref, wo_ref, l1g_ref, l1b_ref,
                x1_ref, m2_ref, wab_ref, wbb_ref, wcb_ref, wob_ref, *, geom, tm, d, alpha):
    i = pl.program_id(0)
    cid = geom.cond_row(i * tm)
    n_heads = d // LANES

    @pl.when(i == 0)
    def _():
        wab_ref[...] = wa_ref[...].astype(BF16)
        wbb_ref[...] = wb_ref[...].astype(BF16)
        wcb_ref[...] = wc_ref[...].astype(BF16)
        wob_ref[...] = wo_ref[...].astype(BF16)

    def tokens(ref):
        return jnp.concatenate([ref[0, h] for h in range(n_heads)], axis=-1).astype(F32)

    mix = tokens(gb_ref) * jnp.dot(ob_ref[...], wbb_ref[...], preferred_element_type=F32)

    vn = vn_ref[...]
    n_ch = tm // CHUNK
    cols = []
    for g in range(n_heads):
        vg = jnp.concatenate([vn[ch * CHUNK:(ch + 1) * CHUNK, g * LANES:(g + 1) * LANES] for ch in range(n_ch)], axis=1)
        mg = jnp.dot(sw_ref[g], vg, preferred_element_type=F32) + sb_ref[g]
        cols.append(jnp.concatenate([mg[:, ch * LANES:(ch + 1) * LANES] for ch in range(n_ch)], axis=0))
    mixed = jnp.concatenate(cols, axis=-1)
    yc = jnp.dot((tokens(u_ref) * mixed).astype(BF16), wcb_ref[...], preferred_element_type=F32)
    mix += tokens(gc_ref) * yc

    conv = jnp.concatenate([r[c] for r in (c0_ref, c1_ref, c2_ref, c3_ref) for c in range(r.shape[0])], axis=-1)
    ha = _silu(_layernorm(conv, clg_ref[...], clb_ref[...]))
    mix += tokens(ga_ref) * jnp.dot(ha.astype(BF16), wab_ref[...], preferred_element_type=F32)

    y = jnp.dot(mix.astype(BF16), wob_ref[...], preferred_element_type=F32)
    g1, sh2, sc2 = _mod_rows(mod_ref, cid, d, (2, 3, 4))
    x1 = _layernorm(alpha * x_ref[...] + g1 * y, l1g_ref[...], l1b_ref[...])
    x1_ref[...] = x1
    m2_ref[...] = (x1 * (1.0 + sc2) + sh2).astype(BF16)


def _mix(geom, convs, p_id, p_sig, vn, ob, x, mod, p, sgu_w, sgu_b, layer, tm, alpha):
    t, d = x.shape
    n_heads = d // LANES
    n_g = sgu_w.shape[0]
    assert tm % CHUNK == 0 and n_g * LANES == d and sum(c.shape[0] for c in convs) == n_heads

    def kind(k):
        return pl.BlockSpec((1, n_heads, tm, LANES), lambda i: (k, 0, i, 0))

    tok = pl.BlockSpec((tm, d), lambda i: (i, 0))
    vec = _layer_spec((1, d), layer)
    mat = _layer_spec((d, d), layer)
    in_specs = [pl.BlockSpec((c.shape[0], tm, LANES), lambda i: (0, i, 0)) for c in convs]
    in_specs += [kind(_ID_U), tok, kind(0), kind(1), kind(2), tok, tok,
                _layer_spec(mod.shape[1:], layer),
                vec, vec, mat, mat,
                _const_spec((n_g, CHUNK, CHUNK)), _const_spec((n_g, CHUNK, 1)), mat, mat, vec, vec]
    return pl.pallas_call(
        functools.partial(_mix_kernel, geom=geom, tm=tm, d=d, alpha=alpha),
        grid=(t // tm,),
        in_specs=in_specs,
        out_specs=[tok, tok],
        out_shape=[jax.ShapeDtypeStruct((t, d), F32), jax.ShapeDtypeStruct((t, d), BF16)],
        scratch_shapes=[pltpu.VMEM((d, d), BF16)] * 4,
        compiler_params=_cparams("arbitrary"),
        name=f"mix{layer}",
    )(*convs, p_id, vn, p_sig, p_sig, p_sig, ob, x, mod,
      p["conv_ln_g"], p["conv_ln_b"], p["w_a_out"], p["w_b_out"],
      sgu_w, sgu_b, p["w_c_out"], p["w_o"], p["ln1_g"], p["ln1_b"])


def _ffn_kernel(m2_ref, x1_ref, mod_ref, modn_ref, perm_ref, wi_ref, wo_ref, g_ref, b_ref, *out_refs,
                geom, tm, d, f_hidden, alpha, last):
    i = pl.program_id(0)
    row0 = i * tm
    cid = geom.cond_row(row0)
    (g2,) = _mod_rows(mod_ref, cid, d, (5,))
    ys = []
    for r0 in range(0, tm, PROJ_SUB_ROWS):
        rs = pl.ds(r0, PROJ_SUB_ROWS)
        m2 = m2_ref[rs, :]
        gte = jnp.dot(m2, wi_ref[:, pl.ds(0, f_hidden)], preferred_element_type=F32)
        up = jnp.dot(m2, wi_ref[:, pl.ds(f_hidden, f_hidden)], preferred_element_type=F32)
        acc = jnp.dot((_silu(gte) * up).astype(BF16), wo_ref[...], preferred_element_type=F32)
        x2 = _layernorm(alpha * x1_ref[rs, :] + g2 * acc, g_ref[...], b_ref[...])
        if last:
            ys.append(_permute_rows(x2, perm_ref[...]))
        else:
            x2_ref, mn_ref = out_refs
            sh1, sc1 = _mod_rows(modn_ref, cid, d, (0, 1))
            x2_ref[rs, :] = x2
            mn_ref[rs, :] = (x2 * (1.0 + sc1) + sh1).astype(BF16)

    if last:
        yc_ref, yl_ref = out_refs
        y = jnp.concatenate(ys, axis=0)

        @pl.when(row0 < geom.t_ctx)
        def _():
            yc_ref[...] = y

        @pl.when(row0 >= geom.t_ctx)
        def _():
            yl_ref[...] = y


def _ffn(geom, m2, x1, mod, w_ffn_in_bf16, w_ffn_out_bf16, ln2_g, ln2_b, layer, tm, alpha):
    t, d = x1.shape
    n_layers, f_hidden, _ = w_ffn_out_bf16.shape
    last = layer + 1 == n_layers
    tok = pl.BlockSpec((tm, d), lambda i: (i, 0))
    if last:
        n_ctx_tiles = geom.t_ctx // tm
        out_specs = [pl.BlockSpec((tm, d), lambda i: (jnp.minimum(i, n_ctx_tiles - 1), 0)),
                     pl.BlockSpec((tm, d), lambda i: (jnp.maximum(i - n_ctx_tiles, 0), 0))]
        out_shape = [jax.ShapeDtypeStruct((geom.t_ctx, d), F32), jax.ShapeDtypeStruct((geom.t_lat, d), F32)]
    else:
        out_specs = [tok, tok]
        out_shape = [jax.ShapeDtypeStruct((t, d), F32), jax.ShapeDtypeStruct((t, d), BF16)]
    return pl.pallas_call(
        functools.partial(_ffn_kernel, geom=geom, tm=tm, d=d, f_hidden=f_hidden, alpha=alpha, last=last),
        grid=(t // tm,),
        in_specs=[tok, tok, _layer_spec(mod.shape[1:], layer),
                  _layer_spec(mod.shape[1:], layer if last else layer + 1),
                  _const_spec((CHUNK, 3 * CHUNK)),
                  _layer_spec((d, 2 * f_hidden), layer), _layer_spec((f_hidden, d), layer),
                  _layer_spec((1, d), layer), _layer_spec((1, d), layer)],
        out_specs=out_specs,
        out_shape=out_shape,
        compiler_params=_cparams("arbitrary"),
        name=f"ffn{layer}",
    )(m2, x1, mod, mod, _row_permutation(False), w_ffn_in_bf16, w_ffn_out_bf16, ln2_g, ln2_b)


def _sincos_2d(n_tokens, dim):
    rows = n_tokens // GRID_W
    nf = dim // 4
    omega = 1.0 / (10000.0 ** (jnp.arange(nf, dtype=F32) / nf))
    ar = jnp.arange(rows).astype(F32)[:, None] * omega
    ac = jnp.arange(GRID_W).astype(F32)[:, None] * omega
    by_row = lambda a: jnp.broadcast_to(a[:, None, :], (rows, GRID_W, nf))
    by_col = lambda a: jnp.broadcast_to(a[None, :, :], (rows, GRID_W, nf))
    table = jnp.concatenate([by_row(jnp.sin(ar)), by_row(jnp.cos(ar)), by_col(jnp.sin(ac)), by_col(jnp.cos(ac))],
                            axis=-1)
    return table.reshape(rows * GRID_W, dim)


def kernel(x_prompt, x_sample, state_hgrn, c, c_ctx, ln_in_g, ln_in_b, w_ada, b_ada, w_in, conv_w, conv_b, conv_ln_g, conv_ln_b, w_a_out, hgrn_lb, hgrn_norm_g, w_b_out, sgu_ln_g, sgu_ln_b, sgu_w, sgu_b, w_c_out, w_o, ln1_g, ln1_b, w_ffn_in, w_ffn_out, ln2_g, ln2_b):
    n_ctx_seq, ctx_len, d = x_prompt.shape
    n_lat_seq, lat_len, _ = x_sample.shape
    n_layers = w_in.shape[0]
    n_heads = state_hgrn.shape[3]
    geom = _Geom(n_ctx_seq, ctx_len, n_lat_seq, lat_len)
    alpha = float((2 * n_layers) ** 0.25)
    assert d == n_heads * LANES and w_in.shape[2] == N_BLK * d
    assert 1 + n_lat_seq <= COND_ROWS and ctx_len % CHUNK == 0 and lat_len % CHUNK == 0

    def row_tile(limit, within_seq):
        tm = CHUNK
        while (2 * tm <= limit and lat_len % (2 * tm) == 0
               and (ctx_len if within_seq else geom.t_ctx) % (2 * tm) == 0):
            tm *= 2
        return tm

    tm_mix = row_tile(256, False)
    tm_ffn = row_tile(512, False)
    tm_proj = row_tile(2048, False)
    n_sub = row_tile(256, True) // CHUNK

    cond = jnp.zeros((COND_ROWS, d), F32).at[0].set(c_ctx).at[1:1 + n_lat_seq].set(c)
    mod = _ada(cond, w_ada, b_ada)

    vec = lambda a: a.reshape(n_layers, 1, d)
    tok = _slab_tokens()
    pos = _to_slab_order(_sincos_2d(lat_len, d))
    x, m = _pre(geom, x_prompt.reshape(geom.t_ctx, d), x_sample.reshape(geom.t_lat, d),
                pos, ln_in_g.reshape(1, d), ln_in_b.reshape(1, d), mod, tm_ffn)

    lvl_f = jnp.asarray(_level_table())
    lvl_b = jnp.asarray(_level_table().T.copy())
    p = dict(conv_ln_g=vec(conv_ln_g), conv_ln_b=vec(conv_ln_b),
             w_a_out=w_a_out, w_b_out=w_b_out, w_c_out=w_c_out, w_o=w_o, ln1_g=vec(ln1_g), ln1_b=vec(ln1_b))
    conv_bv, norm_g, sgu_g, sgu_bb = vec(conv_b), vec(hgrn_norm_g), vec(sgu_ln_g), vec(sgu_ln_b)
    ln2_gv, ln2_bv = vec(ln2_g), vec(ln2_b)
    w_ffn_in_bf16, w_ffn_out_bf16 = w_ffn_in.astype(BF16), w_ffn_out.astype(BF16)
    states = None
    for l in range(n_layers):
        h, vn = _proj_a(geom, m, w_in, sgu_g, sgu_bb, l, tm_ffn)
        (p_id, c0), (p_sig, c1), (p_silu, c2), (p_forget, c3) = (
            _proj_p(geom, m, w_in, hgrn_lb, h, conv_w, conv_bv, l, tm_proj, mode) for mode in _PROJ_MODES)
        o_f, states = _hgrn(geom, p_id, p_silu, p_forget, state_hgrn, lvl_f, reverse=False, layer=l, n_sub=n_sub,
                            states_acc=states)
        ob, states = _hgrn(geom, p_id, p_silu, p_forget, state_hgrn, lvl_b, reverse=True, layer=l, n_sub=n_sub,
                           o_fwd=o_f, norm_g=norm_g, states_acc=states)
        sw = sgu_w[l][:, tok][:, :, tok].astype(BF16)
        sb = sgu_b[l][:, tok][:, :, None]
        x1, m2 = _mix(geom, (c0, c1, c2, c3), p_id, p_sig, vn, ob, x, mod, p, sw, sb, l, tm_mix, alpha)
        x, m = _ffn(geom, m2, x1, mod, w_ffn_in_bf16, w_ffn_out_bf16, ln2_gv, ln2_bv, l, tm_ffn, alpha)

    return x.reshape(x_prompt.shape), m.reshape(x_sample.shape), states
```

```python
import functools

import jax
import jax.numpy as jnp
import numpy as np
from jax import lax
from jax.experimental import pallas as pl
from jax.experimental.pallas import tpu as pltpu

LN_EPS = 1e-5
GRID_W = 64
LANES = 128
SUBLANES = 8
CHUNK = 128
SLABS = CHUNK // SUBLANES
N_LEVELS = 7
COND_ROWS = 8
HGRN_GROUP = 8
PROJ_SUB_ROWS = 256
PROJ_SUB_COLS = 256
VMEM_LIMIT = 56 * 1024 * 1024

F32 = jnp.float32
BF16 = jnp.bfloat16


def _cparams(*sem):
    return pltpu.CompilerParams(dimension_semantics=sem, vmem_limit_bytes=VMEM_LIMIT)


def _sigmoid(x):
    return 1.0 / (1.0 + jnp.exp(-x))


def _silu(x):
    return x * _sigmoid(x)


def _layernorm(x, g, b):
    mu = jnp.mean(x, axis=-1, keepdims=True)
    xc = x - mu
    var = jnp.mean(xc * xc, axis=-1, keepdims=True)
    return xc * lax.rsqrt(var + LN_EPS) * g + b


def _const_spec(shape):
    nd = len(shape)
    return pl.BlockSpec(shape, lambda *_: (0,) * nd, pipeline_mode=pl.Buffered(1))


def _layer_spec(shape, layer):
    nd = len(shape)
    return pl.BlockSpec((None,) + tuple(shape), lambda *_: (layer,) + (0,) * nd, pipeline_mode=pl.Buffered(1))


def _slab_tokens():
    p = np.arange(CHUNK)
    return SLABS * (p % SUBLANES) + p // SUBLANES


def _to_slab_order(x):
    t = x.shape[0]
    rest = x.shape[1:]
    return x.reshape((t // CHUNK, SUBLANES, SLABS) + rest).swapaxes(1, 2).reshape((t,) + rest)


def _ada_kernel(cond_ref, w_ref, b_ref, o_ref):
    s = _silu(cond_ref[...]).astype(BF16)
    o_ref[0] = jnp.dot(s, w_ref[0].astype(BF16), preferred_element_type=F32) + b_ref[0]


def _ada(cond, w_ada, b_ada):
    n_layers, d, n6 = w_ada.shape
    tn = n6 // 4
    return pl.pallas_call(
        _ada_kernel,
        grid=(n_layers, n6 // tn),
        in_specs=[
            pl.BlockSpec((COND_ROWS, d), lambda l, j: (0, 0)),
            pl.BlockSpec((1, d, tn), lambda l, j: (l, 0, j)),
            pl.BlockSpec((1, 1, tn), lambda l, j: (l, 0, j)),
        ],
        out_specs=pl.BlockSpec((1, COND_ROWS, tn), lambda l, j: (l, 0, j)),
        out_shape=jax.ShapeDtypeStruct((n_layers, COND_ROWS, n6), F32),
        compiler_params=_cparams("arbitrary", "arbitrary"),
        name="ada",
    )(cond, w_ada, b_ada.reshape(n_layers, 1, n6))


class _Geom:
    def __init__(self, n_ctx_seq, ctx_len, n_lat_seq, lat_len):
        self.n_ctx_seq, self.ctx_len = n_ctx_seq, ctx_len
        self.n_lat_seq, self.lat_len = n_lat_seq, lat_len
        self.t_ctx = n_ctx_seq * ctx_len
        self.t_lat = n_lat_seq * lat_len
        self.t = self.t_ctx + self.t_lat
        self.n_seq = n_ctx_seq + n_lat_seq

    def cond_row(self, row0):
        return jnp.where(row0 < self.t_ctx, 0, 1 + (row0 - self.t_ctx) // self.lat_len)

    def seq_of(self, row0):
        return jnp.where(row0 < self.t_ctx, row0 // self.ctx_len,
                         self.n_ctx_seq + (row0 - self.t_ctx) // self.lat_len)

    def pos_in_seq(self, row0):
        return jnp.where(row0 < self.t_ctx, row0 % self.ctx_len, (row0 - self.t_ctx) % self.lat_len)

    def seq_len(self, row0):
        return jnp.where(row0 < self.t_ctx, self.ctx_len, self.lat_len)


def _mod_rows(mod_ref, cid, d, idx):
    return [mod_ref[pl.ds(cid, 1), pl.ds(i * d, d)] for i in idx]


def _row_permutation(to_slab_order):
    tok = _slab_tokens()
    p = np.zeros((CHUNK, CHUNK), np.float32)
    p[np.arange(CHUNK), tok] = 1.0
    if not to_slab_order:
        p = p.T
    return jnp.asarray(np.concatenate([p, p, p], axis=1), dtype=BF16)


def _permute_rows(x, perm3):
    out = []
    for c in range(x.shape[0] // CHUNK):
        xc = x[c * CHUNK:(c + 1) * CHUNK, :]
        hi = xc.astype(BF16)
        r1 = xc - hi.astype(F32)
        mid = r1.astype(BF16)
        lo = (r1 - mid.astype(F32)).astype(BF16)
        out.append(jnp.dot(perm3, jnp.concatenate([hi, mid, lo], axis=0), preferred_element_type=F32))
    return jnp.concatenate(out, axis=0)


def _pre_kernel(xc_ref, xl_ref, pos_ref, perm_ref, g_ref, b_ref, mod_ref, x_ref, m_ref, *, geom, tm, d):
    i = pl.program_id(0)
    row0 = i * tm
    cid = geom.cond_row(row0)
    sh1, sc1 = _mod_rows(mod_ref, cid, d, (0, 1))

    def emit(x):
        xn = _layernorm(x, g_ref[...], b_ref[...])
        x_ref[...] = xn
        m_ref[...] = (xn * (1.0 + sc1) + sh1).astype(BF16)

    @pl.when(row0 < geom.t_ctx)
    def _():
        emit(_permute_rows(xc_ref[...], perm_ref[...]))

    @pl.when(row0 >= geom.t_ctx)
    def _():
        emit(_permute_rows(xl_ref[...], perm_ref[...]) + pos_ref[...])


def _pre(geom, x_ctx, x_lat, pos, g, b, mod, tm):
    d = x_ctx.shape[-1]
    n_ctx_tiles = geom.t_ctx // tm
    pos_tiles = geom.lat_len // tm
    return pl.pallas_call(
        functools.partial(_pre_kernel, geom=geom, tm=tm, d=d),
        grid=(geom.t // tm,),
        in_specs=[
            pl.BlockSpec((tm, d), lambda i: (jnp.minimum(i, n_ctx_tiles - 1), 0)),
            pl.BlockSpec((tm, d), lambda i: (jnp.maximum(i - n_ctx_tiles, 0), 0)),
            pl.BlockSpec((tm, d), lambda i: (jnp.maximum(i - n_ctx_tiles, 0) % pos_tiles, 0)),
            _const_spec((CHUNK, 3 * CHUNK)),
            _const_spec((1, d)), _const_spec((1, d)), _layer_spec(mod.shape[1:], 0),
        ],
        out_specs=[pl.BlockSpec((tm, d), lambda i: (i, 0))] * 2,
        out_shape=[jax.ShapeDtypeStruct((geom.t, d), F32), jax.ShapeDtypeStruct((geom.t, d), BF16)],
        compiler_params=_cparams("arbitrary"),
        name="pre",
    )(x_ctx, x_lat, pos, _row_permutation(True), g, b, mod)


_W_VAL, _W_GATE, _W_Q, _W_ZF, _W_ZB, _W_I, _W_G, _W_U, _W_V, _W_GA, _W_GB, _W_GC = range(12)
N_BLK = 12
_PROJ_MODES = {
    "id": ((_W_U, _W_Q, _W_I), 0),
    "sig": ((_W_GA, _W_GB, _W_GC), 3),
    "silu": ((_W_G,), 6),
    "forget": ((_W_ZF, _W_ZB), 7),
}
_ID_U, _ID_Q, _ID_I = range(3)
_F_FWD, _F_BWD = range(2)


def _blk_lookup(blks, s):
    out = blks[-1]
    for k in range(len(blks) - 2, -1, -1):
        out = jnp.where(s == k, blks[k], out)
    return out


def _proj_a_kernel(m_ref, wv_ref, wg_ref, ws_ref, g_ref, b_ref, h_ref, vn_ref, wvb_ref, wgb_ref, wsb_ref):
    @pl.when(pl.program_id(0) == 0)
    def _():
        wvb_ref[...] = wv_ref[...].astype(BF16)
        wgb_ref[...] = wg_ref[...].astype(BF16)
        wsb_ref[...] = ws_ref[...].astype(BF16)

    for r0 in range(0, m_ref.shape[0], PROJ_SUB_ROWS):
        rs = pl.ds(r0, PROJ_SUB_ROWS)
        m = m_ref[rs, :]
        val = jnp.dot(m, wvb_ref[...], preferred_element_type=F32)
        gate = jnp.dot(m, wgb_ref[...], preferred_element_type=F32)
        hv = (val * _sigmoid(gate)).astype(BF16)
        for hh in range(h_ref.shape[0]):
            h_ref[hh, rs, :] = hv[:, hh * LANES:(hh + 1) * LANES]
        v = jnp.dot(m, wsb_ref[...], preferred_element_type=F32)
        vn_ref[rs, :] = _layernorm(v, g_ref[...], b_ref[...]).astype(BF16)


def _proj_a(geom, m, w_in, sgu_g, sgu_b, layer, tm):
    d = m.shape[-1]
    tok = pl.BlockSpec((tm, d), lambda i: (i, 0))

    def wblk(b):
        return pl.BlockSpec((None, d, d), lambda i: (layer, 0, b), pipeline_mode=pl.Buffered(1))

    return pl.pallas_call(
        _proj_a_kernel,
        grid=(geom.t // tm,),
        in_specs=[tok, wblk(_W_VAL), wblk(_W_GATE), wblk(_W_V), _layer_spec((1, d), layer), _layer_spec((1, d), layer)],
        out_specs=[pl.BlockSpec((d // LANES, tm, LANES), lambda i: (0, i, 0)), tok],
        out_shape=[jax.ShapeDtypeStruct((d // LANES, geom.t, LANES), BF16), jax.ShapeDtypeStruct((geom.t, d), BF16)],
        scratch_shapes=[pltpu.VMEM((d, d), BF16)] * 3,
        compiler_params=_cparams("arbitrary"),
        name=f"proj_a{layer}",
    )(m, w_in, w_in, w_in, sgu_g, sgu_b)


def _load_slabs(ref, c):
    out = []
    for p in range(SLABS // 2):
        x = ref[pl.ds(c * CHUNK + 2 * p * SUBLANES, 2 * SUBLANES), :].astype(F32)
        out += [x[:SUBLANES], x[SUBLANES:]]
    return out


def _conv_chunk(x, prev, nxt, w_ref, bias, has_prev, has_next):
    conv_k = w_ref.shape[0]
    half = conv_k // 2
    riota = lax.broadcasted_iota(jnp.int32, (SUBLANES, LANES), 0)
    up, down = [], []
    for j in range(SLABS):
        first_next = jnp.where(has_next, nxt[j][0:1, :], 0.0)
        last_prev = jnp.where(has_prev, prev[j][SUBLANES - 1:SUBLANES, :], 0.0)
        up.append(jnp.where(riota == SUBLANES - 1, first_next, pltpu.roll(x[j], SUBLANES - 1, 0)))
        down.append(jnp.where(riota == 0, last_prev, pltpu.roll(x[j], 1, 0)))
    group = SLABS // 2
    outs = []
    for j0 in range(0, SLABS, group):
        accs = [bias] * group
        for k in range(conv_k):
            wk = jnp.broadcast_to(w_ref[k:k + 1, :], (SUBLANES, LANES))
            for jj in range(group):
                s = j0 + jj + k - half
                term = x[s] if 0 <= s < SLABS else (up[s - SLABS] if s >= SLABS else down[s + SLABS])
                accs[jj] = accs[jj] + term * wk
        outs += accs
    return outs


def _proj_p_kernel(m_ref, w_ref, lb_ref, h_ref, hp_ref, hn_ref, cw_ref, cb_ref, o_ref, conv_ref, wb_ref,
                   *, geom, layer, n_heads, mode, conv_rows, conv_split):
    k = pl.program_id(0)
    i = pl.program_id(1)

    @pl.when(i == 0)
    def _():
        wb_ref[...] = w_ref[...].astype(BF16)

    if mode == "forget":
        lb_all = lb_ref[...]
        e = jnp.exp(lb_all - jnp.max(lb_all, axis=0, keepdims=True))
        soft = e / jnp.sum(e, axis=0, keepdims=True)
        lb = jnp.sum(soft[:layer + 1], axis=0, keepdims=True) - soft[0:1]
    epilogue = {"id": lambda a: a, "sig": _sigmoid, "silu": _silu, "forget": None}[mode]

    tm, d = m_ref.shape
    n_conv = conv_rows // CHUNK
    row0 = (i * conv_split + k) * conv_rows if conv_split > 1 else i * conv_rows
    pieces = [(r0, n0) for r0 in range(0, tm, PROJ_SUB_ROWS) for n0 in range(0, d, PROJ_SUB_COLS)]
    conv_before = {c * len(pieces) // n_conv: c for c in range(n_conv)}
    for p, (r0, n0) in enumerate(pieces):
        if p in conv_before:
            c = conv_before[p]
            pos = geom.pos_in_seq(row0 + c * CHUNK)
            x = _load_slabs(h_ref, c)
            prev = _load_slabs(h_ref, c - 1) if c > 0 else _load_slabs(hp_ref, 0)
            nxt = _load_slabs(h_ref, c + 1) if c + 1 < n_conv else _load_slabs(hn_ref, 0)
            outs = _conv_chunk(x, prev, nxt, cw_ref, cb_ref[...], pos > 0,
                               pos + CHUNK < geom.seq_len(row0 + c * CHUNK))
            for j, o in enumerate(outs):
                conv_ref[pl.ds(c * CHUNK + j * SUBLANES, SUBLANES), :] = o

        rs = pl.ds(r0, PROJ_SUB_ROWS)
        y = jnp.dot(m_ref[rs, :], wb_ref[:, pl.ds(n0, PROJ_SUB_COLS)], preferred_element_type=F32)
        y = epilogue(y) if mode != "forget" else lb[:, n0:n0 + PROJ_SUB_COLS] + (
            1.0 - lb[:, n0:n0 + PROJ_SUB_COLS]) * _sigmoid(y)
        for hh in range(PROJ_SUB_COLS // LANES):
            o_ref[0, n0 // LANES + hh, rs, :] = y[:, hh * LANES:(hh + 1) * LANES].astype(o_ref.dtype)


def _proj_p(geom, m, w_in, hgrn_lb, h, conv_w, conv_b, layer, tm, mode):
    d = m.shape[-1]
    n_heads = d // LANES
    blks, first_col = _PROJ_MODES[mode]
    n_kind = len(blks)
    conv_k = conv_w.shape[1]
    assert conv_k // 2 < SLABS
    if mode == "forget":
        assert first_col == n_heads - 1
        conv_split, conv_cols, conv_rows = n_kind, 1, tm // n_kind
        lane_col = lambda k: first_col
        row_blk = lambda k, i: i * conv_split + k
        out_col = lambda k: 0
    else:
        conv_split, conv_cols, conv_rows = 1, n_kind, tm
        lane_col = lambda k: first_col + k
        row_blk = lambda k, i: i
        out_col = lambda k: k
    cpr = conv_rows // CHUNK
    n_chunks = geom.t // CHUNK
    prev_chunk = lambda k, i: jnp.maximum(row_blk(k, i) * cpr - 1, 0)
    next_chunk = lambda k, i: jnp.minimum((row_blk(k, i) + 1) * cpr, n_chunks - 1)
    return pl.pallas_call(
        functools.partial(_proj_p_kernel, geom=geom, layer=layer, n_heads=n_heads, mode=mode,
                          conv_rows=conv_rows, conv_split=conv_split),
        grid=(n_kind, geom.t // tm),
        in_specs=[
            pl.BlockSpec((tm, d), lambda k, i: (i, 0)),
            pl.BlockSpec((None, d, d), lambda k, i: (layer, 0, _blk_lookup(blks, k))),
            _const_spec(hgrn_lb.shape),
            pl.BlockSpec((None, conv_rows, LANES), lambda k, i: (lane_col(k), row_blk(k, i), 0)),
            pl.BlockSpec((None, CHUNK, LANES), lambda k, i: (lane_col(k), prev_chunk(k, i), 0)),
            pl.BlockSpec((None, CHUNK, LANES), lambda k, i: (lane_col(k), next_chunk(k, i), 0)),
            pl.BlockSpec((None, conv_k, LANES), lambda k, i: (layer, 0, lane_col(k))),
            pl.BlockSpec((None, 1, LANES), lambda k, i: (layer, 0, lane_col(k))),
        ],
        out_specs=[pl.BlockSpec((1, n_heads, tm, LANES), lambda k, i: (k, 0, i, 0)),
                   pl.BlockSpec((None, conv_rows, LANES), lambda k, i: (out_col(k), row_blk(k, i), 0))],
        out_shape=[jax.ShapeDtypeStruct((n_kind, n_heads, geom.t, LANES), F32 if mode == "forget" else BF16),
                   jax.ShapeDtypeStruct((conv_cols, geom.t, LANES), F32)],
        scratch_shapes=[pltpu.VMEM((d, d), BF16)],
        compiler_params=_cparams("arbitrary", "arbitrary"),
        name=f"proj_{mode}{layer}",
    )(m, w_in, hgrn_lb, h, h, h, conv_w, conv_b)


def _level_table():
    tok = _slab_tokens()
    t, s = tok[:, None], tok[None, :]
    x = t ^ s
    lvl = np.zeros((CHUNK, CHUNK), np.int32)
    for k in range(N_LEVELS):
        lvl = np.where((x >> k) > 0, k + 1, lvl)
    lvl = np.where(s > t, -1, lvl)
    return lvl.astype(np.int32)


def _row_bcast(x, r):
    return jnp.broadcast_to(x[r:r + 1, :], x.shape)


def _sublane_gather(src, rows_from, riota):
    out = jnp.ones_like(src)
    for r, rf in enumerate(rows_from):
        if rf is not None:
            out = jnp.where(riota == r, _row_bcast(src, rf), out)
    return out


def _cat_bf16(slabs):
    return jnp.concatenate(slabs, axis=0).astype(BF16)


class _HeadChunk:
    def __init__(self, q, f, lvl, reverse):
        self.lvl, self.reverse = lvl, reverse
        kk = [1.0 - fj for fj in f]
        self.tgt = list(f)
        self.qt = [q[j] * f[j] for j in range(SLABS)]
        self.ks = kk
        self.edge_slab = 0 if reverse else SLABS - 1
        diag = jnp.sum(jnp.concatenate([q[j] * kk[j] for j in range(SLABS)], axis=0), axis=-1, keepdims=True)
        self.a = jnp.where(lvl == 0, diag, 0.0)

    def level(self, k):
        tgt, reverse = self.tgt, self.reverse
        p = lax.dot_general(_cat_bf16(self.qt), _cat_bf16(self.ks), (((1,), (1,)), ((), ())),
                            preferred_element_type=F32)
        self.a = jnp.where(self.lvl == k + 1, p, self.a)
        if k < 4:
            bit, low = 1 << k, (1 << k) - 1
            new_tgt, new_qt, new_ks = list(tgt), list(self.qt), list(self.ks)
            for j in range(SLABS):
                in_far_half = bool(j & bit) != reverse
                if in_far_half:
                    sib_edge = ((j & ~bit) | low) if not reverse else ((j | bit) & ~low)
                    new_qt[j] = self.qt[j] * tgt[sib_edge]
                    below = j & (2 * bit - 1)
                    if below == (0 if reverse else 2 * bit - 1):
                        new_tgt[j] = tgt[j] * tgt[sib_edge]
                else:
                    sib_edge = ((j | bit) | low) if not reverse else ((j & ~bit) & ~low)
                    new_ks[j] = self.ks[j] * tgt[sib_edge]
            self.tgt, self.qt, self.ks = new_tgt, new_qt, new_ks
        else:
            bit, low = 1 << (k - 4), (1 << (k - 4)) - 1
            riota = lax.broadcasted_iota(jnp.int32, (SUBLANES, LANES), 0)
            edge = tgt[self.edge_slab]
            rows_t, rows_s = [], []
            for r in range(SUBLANES):
                in_far_half = bool(r & bit) != reverse
                if in_far_half:
                    rows_t.append(((r & ~bit) | low) if not reverse else ((r | bit) & ~low))
                    rows_s.append(None)
                else:
                    rows_t.append(None)
                    rows_s.append(((r | bit) | low) if not reverse else ((r & ~bit) & ~low))
            fac_t = _sublane_gather(edge, rows_t, riota)
            fac_s = _sublane_gather(edge, rows_s, riota)
            self.qt = [x * fac_t for x in self.qt]
            self.ks = [x * fac_s for x in self.ks]
            self.tgt = list(tgt)
            self.tgt[self.edge_slab] = edge * fac_t

    def finish(self, v_cat, st):
        lhs = jnp.concatenate([self.a.astype(BF16), _cat_bf16(self.qt)], axis=1)
        rhs = jnp.concatenate([v_cat, st.astype(BF16)], axis=0)
        o = jnp.dot(lhs, rhs, preferred_element_type=F32)
        row = 0 if self.reverse else SUBLANES - 1
        edge = self.tgt[self.edge_slab]
        total = jnp.transpose(jnp.broadcast_to(edge[row:row + 1, :], (LANES, LANES)))
        upd = lax.dot_general(_cat_bf16(self.ks), v_cat, (((0,), (0,)), ((), ())), preferred_element_type=F32)
        return o, st * total + upd


def _hgrn_kernel(*refs, geom, reverse, n_heads, n_sub):
    *ins, _, o_ref, sf_ref, st_ref = refs
    if reverse:
        q_ref, v_ref, sg_ref, f_ref, of_ref, ng_ref, lvl_ref, s0_ref = ins
    else:
        q_ref, v_ref, f_ref, lvl_ref, s0_ref = ins
    rows = n_sub * CHUNK
    c = pl.program_id(0)
    n_steps = geom.t // rows
    ce = (n_steps - 1 - c) if reverse else c
    row0 = ce * rows
    pos = geom.pos_in_seq(row0)
    at_start = pos == 0
    at_end = pos == geom.seq_len(row0) - rows
    is_first, is_last = (at_end, at_start) if reverse else (at_start, at_end)

    is_ctx = row0 < geom.t_ctx

    @pl.when(is_first)
    def _():
        st_ref[...] = jnp.where(is_ctx, 0.0, s0_ref[...])

    lvl = lvl_ref[...]
    slab = lambda x: [x[j * SUBLANES:(j + 1) * SUBLANES, :] for j in range(SLABS)]
    for cc in (reversed(range(n_sub)) if reverse else range(n_sub)):
        rs = pl.ds(cc * CHUNK, CHUNK)
        for h0 in range(0, n_heads, HGRN_GROUP):
            heads = range(h0, min(h0 + HGRN_GROUP, n_heads))
            work = [_HeadChunk(slab(q_ref[0, h, rs, :].astype(F32)), slab(f_ref[0, h, rs, :]), lvl, reverse)
                    for h in heads]
            for k in range(N_LEVELS):
                for w in work:
                    w.level(k)
            for h, w in zip(heads, work):
                o, st_new = w.finish(v_ref[0, h, rs, :], st_ref[h])
                st_ref[h] = st_new
                if reverse:
                    o = o + of_ref[h, rs, :]
                    o = o * lax.rsqrt(jnp.mean(o * o, axis=-1, keepdims=True) + LN_EPS)
                    cols = pl.ds(h * LANES, LANES)
                    o_ref[rs, cols] = (o * ng_ref[:, cols] * sg_ref[0, h, rs, :].astype(F32)).astype(BF16)
                else:
                    o_ref[h, rs, :] = o

    @pl.when(jnp.logical_and(is_last, is_ctx))
    def _():
        sf_ref[0] = st_ref[...]


def _hgrn(geom, p_id, p_silu, p_forget, state_hgrn, lvl, states_acc, *, reverse, layer, n_sub, o_fwd=None,
          norm_g=None):
    _, n_heads, t, _ = p_id.shape
    n_layers = state_hgrn.shape[1]
    d = n_heads * LANES
    rows = n_sub * CHUNK
    n_steps = t // rows

    def step(c):
        return (n_steps - 1 - c) if reverse else c

    def blk(kind):
        return pl.BlockSpec((1, n_heads, rows, LANES), lambda c: (kind, 0, step(c), 0))

    head_rows = pl.BlockSpec((n_heads, rows, LANES), lambda c: (0, step(c), 0))
    seq = lambda c: geom.seq_of(step(c) * rows)
    direction = 1 if reverse else 0
    s0_spec = pl.BlockSpec((None, None, None, n_heads, LANES, LANES),
                           lambda c: (jnp.maximum(seq(c) - geom.n_ctx_seq, 0), layer, direction, 0, 0, 0))
    state_spec = pl.BlockSpec((1, None, None, n_heads, LANES, LANES),
                              lambda c: (jnp.minimum(seq(c), geom.n_ctx_seq - 1), layer, direction, 0, 0, 0))
    lvl_spec = _const_spec((CHUNK, CHUNK))
    state_shape = jax.ShapeDtypeStruct((geom.n_ctx_seq, n_layers, 2, n_heads, LANES, LANES), F32)
    if reverse:
        in_specs = [blk(_ID_Q), blk(_ID_I), blk(0), blk(_F_BWD), head_rows, _layer_spec((1, d), layer),
                    lvl_spec, s0_spec]
        args = (p_id, p_id, p_silu, p_forget, o_fwd, norm_g, lvl, state_hgrn)
        out_specs = [pl.BlockSpec((rows, d), lambda c: (step(c), 0)), state_spec]
        out_shape = [jax.ShapeDtypeStruct((t, d), BF16), state_shape]
    else:
        in_specs = [blk(_ID_Q), blk(_ID_I), blk(_F_FWD), lvl_spec, s0_spec]
        args = (p_id, p_id, p_forget, lvl, state_hgrn)
        out_specs = [head_rows, state_spec]
        out_shape = [jax.ShapeDtypeStruct((n_heads, t, LANES), F32), state_shape]
    return pl.pallas_call(
        functools.partial(_hgrn_kernel, geom=geom, reverse=reverse, n_heads=n_heads, n_sub=n_sub),
        grid=(n_steps,),
        in_specs=in_specs + [pl.BlockSpec(memory_space=pl.ANY)],
        out_specs=out_specs,
        out_shape=out_shape,
        input_output_aliases={len(args): 1},
        scratch_shapes=[pltpu.VMEM((n_heads, LANES, LANES), F32)],
        compiler_params=_cparams("arbitrary"),
        name=f"hgrn{layer}{'b' if reverse else 'f'}",
    )(*args, states_acc)


def _mix_kernel(c0_ref, c1_ref, c2_ref, c3_ref, u_ref, vn_ref, ga_ref, gb_ref, gc_ref, ob_ref, x_ref, mod_ref,
                clg_ref, clb_ref, wa_ref, wb_ref, sw_ref, sb_ref, wc_ref, wo_ref, l1g_ref, l1b_ref,
                x1_ref, m2_ref, wab_ref, wbb_ref, wcb_ref, wob_ref, *, geom, tm, d, alpha):
    i = pl.program_id(0)
    cid = geom.cond_row(i * tm)
    n_heads = d // LANES

    @pl.when(i == 0)
    def _():
        wab_ref[...] = wa_ref[...].astype(BF16)
        wbb_ref[...] = wb_ref[...].astype(BF16)
        wcb_ref[...] = wc_ref[...].astype(BF16)
        wob_ref[...] = wo_ref[...].astype(BF16)

    def tokens(ref):
        return jnp.concatenate([ref[0, h] for h in range(n_heads)], axis=-1).astype(F32)

    mix = tokens(gb_ref) * jnp.dot(ob_ref[...], wbb_ref[...], preferred_element_type=F32)

    vn = vn_ref[...]
    n_ch = tm // CHUNK
    cols = []
    for g in range(n_heads):
        vg = jnp.concatenate([vn[ch * CHUNK:(ch + 1) * CHUNK, g * LANES:(g + 1) * LANES] for ch in range(n_ch)], axis=1)
        mg = jnp.dot(sw_ref[g], vg, preferred_element_type=F32) + sb_ref[g]
        cols.append(jnp.concatenate([mg[:, ch * LANES:(ch + 1) * LANES] for ch in range(n_ch)], axis=0))
    mixed = jnp.concatenate(cols, axis=-1)
    yc = jnp.dot((tokens(u_ref) * mixed).astype(BF16), wcb_ref[...], preferred_element_type=F32)
    mix += tokens(gc_ref) * yc

    conv = jnp.concatenate([r[c] for r in (c0_ref, c1_ref, c2_ref, c3_ref) for c in range(r.shape[0])], axis=-1)
    ha = _silu(_layernorm(conv, clg_ref[...], clb_ref[...]))
    mix += tokens(ga_ref) * jnp.dot(ha.astype(BF16), wab_ref[...], preferred_element_type=F32)

    y = jnp.dot(mix.astype(BF16), wob_ref[...], preferred_element_type=F32)
    g1, sh2, sc2 = _mod_rows(mod_ref, cid, d, (2, 3, 4))
    x1 = _layernorm(alpha * x_ref[...] + g1 * y, l1g_ref[...], l1b_ref[...])
    x1_ref[...] = x1
    m2_ref[...] = (x1 * (1.0 + sc2) + sh2).astype(BF16)


def _mix(geom, convs, p_id, p_sig, vn, ob, x, mod, p, sgu_w, sgu_b, layer, tm, alpha):
    t, d = x.shape
    n_heads = d // LANES
    n_g = sgu_w.shape[0]
    assert tm % CHUNK == 0 and n_g * LANES == d and sum(c.shape[0] for c in convs) == n_heads

    def kind(k):
        return pl.BlockSpec((1, n_heads, tm, LANES), lambda i: (k, 0, i, 0))

    tok = pl.BlockSpec((tm, d), lambda i: (i, 0))
    vec = _layer_spec((1, d), layer)
    mat = _layer_spec((d, d), layer)
    in_specs = [pl.BlockSpec((c.shape[0], tm, LANES), lambda i: (0, i, 0)) for c in convs]
    in_specs += [kind(_ID_U), tok, kind(0), kind(1), kind(2), tok, tok,
                _layer_spec(mod.shape[1:], layer),
                vec, vec, mat, mat,
                _const_spec((n_g, CHUNK, CHUNK)), _const_spec((n_g, CHUNK, 1)), mat, mat, vec, vec]
    return pl.pallas_call(
        functools.partial(_mix_kernel, geom=geom, tm=tm, d=d, alpha=alpha),
        grid=(t // tm,),
        in_specs=in_specs,
        out_specs=[tok, tok],
        out_shape=[jax.ShapeDtypeStruct((t, d), F32), jax.ShapeDtypeStruct((t, d), BF16)],
        scratch_shapes=[pltpu.VMEM((d, d), BF16)] * 4,
        compiler_params=_cparams("arbitrary"),
        name=f"mix{layer}",
    )(*convs, p_id, vn, p_sig, p_sig, p_sig, ob, x, mod,
      p["conv_ln_g"], p["conv_ln_b"], p["w_a_out"], p["w_b_out"],
      sgu_w, sgu_b, p["w_c_out"], p["w_o"], p["ln1_g"], p["ln1_b"])


def _ffn_kernel(m2_ref, x1_ref, mod_ref, modn_ref, perm_ref, wi_ref, wo_ref, g_ref, b_ref, *out_refs,
                geom, tm, d, f_hidden, alpha, last):
    i = pl.program_id(0)
    row0 = i * tm
    cid = geom.cond_row(row0)
    (g2,) = _mod_rows(mod_ref, cid, d, (5,))
    ys = []
    for r0 in range(0, tm, PROJ_SUB_ROWS):
        rs = pl.ds(r0, PROJ_SUB_ROWS)
        m2 = m2_ref[rs, :]
        gte = jnp.dot(m2, wi_ref[:, pl.ds(0, f_hidden)], preferred_element_type=F32)
        up = jnp.dot(m2, wi_ref[:, pl.ds(f_hidden, f_hidden)], preferred_element_type=F32)
        acc = jnp.dot((_silu(gte) * up).astype(BF16), wo_ref[...], preferred_element_type=F32)
        x2 = _layernorm(alpha * x1_ref[rs, :] + g2 * acc, g_ref[...], b_ref[...])
        if last:
            ys.append(_permute_rows(x2, perm_ref[...]))
        else:
            x2_ref, mn_ref = out_refs
            sh1, sc1 = _mod_rows(modn_ref, cid, d, (0, 1))
            x2_ref[rs, :] = x2
            mn_ref[rs, :] = (x2 * (1.0 + sc1) + sh1).astype(BF16)

    if last:
        yc_ref, yl_ref = out_refs
        y = jnp.concatenate(ys, axis=0)

        @pl.when(row0 < geom.t_ctx)
        def _():
            yc_ref[...] = y

        @pl.when(row0 >= geom.t_ctx)
        def _():
            yl_ref[...] = y


def _ffn(geom, m2, x1, mod, w_ffn_in_bf16, w_ffn_out_bf16, ln2_g, ln2_b, layer, tm, alpha):
    t, d = x1.shape
    n_layers, f_hidden, _ = w_ffn_out_bf16.shape
    last = layer + 1 == n_layers
    tok = pl.BlockSpec((tm, d), lambda i: (i, 0))
    if last:
        n_ctx_tiles = geom.t_ctx // tm
        out_specs = [pl.BlockSpec((tm, d), lambda i: (jnp.minimum(i, n_ctx_tiles - 1), 0)),
                     pl.BlockSpec((tm, d), lambda i: (jnp.maximum(i - n_ctx_tiles, 0), 0))]
        out_shape = [jax.ShapeDtypeStruct((geom.t_ctx, d), F32), jax.ShapeDtypeStruct((geom.t_lat, d), F32)]
    else:
        out_specs = [tok, tok]
        out_shape = [jax.ShapeDtypeStruct((t, d), F32), jax.ShapeDtypeStruct((t, d), BF16)]
    return pl.pallas_call(
        functools.partial(_ffn_kernel, geom=geom, tm=tm, d=d, f_hidden=f_hidden, alpha=alpha, last=last),
        grid=(t // tm,),
        in_specs=[tok, tok, _layer_spec(mod.shape[1:], layer),
                  _layer_spec(mod.shape[1:], layer if last else layer + 1),
                  _const_spec((CHUNK, 3 * CHUNK)),
                  _layer_spec((d, 2 * f_hidden), layer), _layer_spec((f_hidden, d), layer),
                  _layer_spec((1, d), layer), _layer_spec((1, d), layer)],
        out_specs=out_specs,
        out_shape=out_shape,
        compiler_params=_cparams("arbitrary"),
        name=f"ffn{layer}",
    )(m2, x1, mod, mod, _row_permutation(False), w_ffn_in_bf16, w_ffn_out_bf16, ln2_g, ln2_b)


def _sincos_2d(n_tokens, dim):
    rows = n_tokens // GRID_W
    nf = dim // 4
    omega = 1.0 / (10000.0 ** (jnp.arange(nf, dtype=F32) / nf))
    ar = jnp.arange(rows).astype(F32)[:, None] * omega
    ac = jnp.arange(GRID_W).astype(F32)[:, None] * omega
    by_row = lambda a: jnp.broadcast_to(a[:, None, :], (rows, GRID_W, nf))
    by_col = lambda a: jnp.broadcast_to(a[None, :, :], (rows, GRID_W, nf))
    table = jnp.concatenate([by_row(jnp.sin(ar)), by_row(jnp.cos(ar)), by_col(jnp.sin(ac)), by_col(jnp.cos(ac))],
                            axis=-1)
    return table.reshape(rows * GRID_W, dim)


def kernel(x_prompt, x_sample, state_hgrn, c, c_ctx, ln_in_g, ln_in_b, w_ada, b_ada, w_in, conv_w, conv_b, conv_ln_g, conv_ln_b, w_a_out, hgrn_lb, hgrn_norm_g, w_b_out, sgu_ln_g, sgu_ln_b, sgu_w, sgu_b, w_c_out, w_o, ln1_g, ln1_b, w_ffn_in, w_ffn_out, ln2_g, ln2_b):
    n_ctx_seq, ctx_len, d = x_prompt.shape
    n_lat_seq, lat_len, _ = x_sample.shape
    n_layers = w_in.shape[0]
    n_heads = state_hgrn.shape[3]
    geom = _Geom(n_ctx_seq, ctx_len, n_lat_seq, lat_len)
    alpha = float((2 * n_layers) ** 0.25)
    assert d == n_heads * LANES and w_in.shape[2] == N_BLK * d
    assert 1 + n_lat_seq <= COND_ROWS and ctx_len % CHUNK == 0 and lat_len % CHUNK == 0

    def row_tile(limit, within_seq):
        tm = CHUNK
        while (2 * tm <= limit and lat_len % (2 * tm) == 0
               and (ctx_len if within_seq else geom.t_ctx) % (2 * tm) == 0):
            tm *= 2
        return tm

    tm_mix = row_tile(256, False)
    tm_ffn = row_tile(512, False)
    tm_proj = row_tile(2048, False)
    n_sub = row_tile(256, True) // CHUNK

    cond = jnp.zeros((COND_ROWS, d), F32).at[0].set(c_ctx).at[1:1 + n_lat_seq].set(c)
    mod = _ada(cond, w_ada, b_ada)

    vec = lambda a: a.reshape(n_layers, 1, d)
    tok = _slab_tokens()
    pos = _to_slab_order(_sincos_2d(lat_len, d))
    x, m = _pre(geom, x_prompt.reshape(geom.t_ctx, d), x_sample.reshape(geom.t_lat, d),
                pos, ln_in_g.reshape(1, d), ln_in_b.reshape(1, d), mod, tm_ffn)

    lvl_f = jnp.asarray(_level_table())
    lvl_b = jnp.asarray(_level_table().T.copy())
    p = dict(conv_ln_g=vec(conv_ln_g), conv_ln_b=vec(conv_ln_b),
             w_a_out=w_a_out, w_b_out=w_b_out, w_c_out=w_c_out, w_o=w_o, ln1_g=vec(ln1_g), ln1_b=vec(ln1_b))
    conv_bv, norm_g, sgu_g, sgu_bb = vec(conv_b), vec(hgrn_norm_g), vec(sgu_ln_g), vec(sgu_ln_b)
    ln2_gv, ln2_bv = vec(ln2_g), vec(ln2_b)
    w_ffn_in_bf16, w_ffn_out_bf16 = w_ffn_in.astype(BF16), w_ffn_out.astype(BF16)
    states = jnp.zeros((n_ctx_seq, n_layers, 2, n_heads, LANES, LANES), F32)
    for l in range(n_layers):
        h, vn = _proj_a(geom, m, w_in, sgu_g, sgu_bb, l, tm_ffn)
        (p_id, c0), (p_sig, c1), (p_silu, c2), (p_forget, c3) = (
            _proj_p(geom, m, w_in, hgrn_lb, h, conv_w, conv_bv, l, tm_proj, mode) for mode in _PROJ_MODES)
        o_f, states = _hgrn(geom, p_id, p_silu, p_forget, state_hgrn, lvl_f, states, reverse=False, layer=l,
                            n_sub=n_sub)
        ob, states = _hgrn(geom, p_id, p_silu, p_forget, state_hgrn, lvl_b, states, reverse=True, layer=l,
                           n_sub=n_sub, o_fwd=o_f, norm_g=norm_g)
        sw = sgu_w[l][:, tok][:, :, tok].astype(BF16)
        sb = sgu_b[l][:, tok][:, :, None]
        x1, m2 = _mix(geom, (c0, c1, c2, c3), p_id, p_sig, vn, ob, x, mod, p, sw, sb, l, tm_mix, alpha)
        x, m = _ffn(geom, m2, x1, mod, w_ffn_in_bf16, w_ffn_out_bf16, ln2_gv, ln2_bv, l, tm_ffn, alpha)

    return x.reshape(x_prompt.shape), m.reshape(x_sample.shape), states
```
